```python
import jax, jax.numpy as jnp
from jax import lax
import numpy as np

D_MODEL = 1024
BATCH = 4
SEQ = 8192
DEPTH = 2

GRID_W = 64
CTX_LEN = 256
ROPE_THETA = 10000.0
LN_EPS = 1e-6
NEG_INF = -1e30
DN_ALPHA = (2 * DEPTH) ** 0.25
DN_BETA = (8 * DEPTH) ** -0.25
N_EVEN = (DEPTH + 1) // 2
N_ODD = DEPTH // 2

LRU_WIDTH = 512
LRU_BLOCKS = 8
LRU_BLOCK_DIM = LRU_WIDTH // LRU_BLOCKS
LRU_CONV_W = 4
LRU_C = 8.0
SWA_HEADS = 8
SWA_KV_HEADS = 2
SWA_HEAD_DIM = 64
WINDOW = 128
Q_BLOCK = 128
EVEN_IN = 2 * LRU_WIDTH + (SWA_HEADS + 2 * SWA_KV_HEADS) * SWA_HEAD_DIM
EVEN_MIX = LRU_WIDTH + SWA_HEADS * SWA_HEAD_DIM
MLA_HEADS = 8
MLA_Q_RANK = 256
MLA_KV_RANK = 128
MLA_NOPE = 64
MLA_ROPE = 32
MLA_V = 64
CONF_CH = 512
CONF_K = 31
ODD_IN = MLA_Q_RANK + MLA_KV_RANK + MLA_ROPE + 2 * CONF_CH
ODD_MIX = MLA_HEADS * MLA_V + CONF_CH
N_GROUPS = 4
EXPERTS_PER_GROUP = 8
N_EXPERTS = N_GROUPS * EXPERTS_PER_GROUP
TOP_K = 2
D_EXPERT = 512
MOE_BLOCK = 128

kernel_name = "hybrid_rglru_swa_mla_conformer_hmoe_dit"


def _layer_norm(x, g, b):
    xf = x.astype(jnp.float32)
    mu = jnp.mean(xf, axis=-1, keepdims=True)
    var = jnp.mean(jnp.square(xf - mu), axis=-1, keepdims=True)
    return ((xf - mu) * lax.rsqrt(var + LN_EPS) * g + b).astype(x.dtype)


def _rms_norm(x, g):
    xf = x.astype(jnp.float32)
    return (xf * lax.rsqrt(jnp.mean(jnp.square(xf), axis=-1, keepdims=True) + LN_EPS) * g).astype(x.dtype)


def _grid_positions(n):
    rows = n // GRID_W
    row = jnp.repeat(jnp.arange(rows, dtype=jnp.int32), GRID_W)
    col = jnp.tile(jnp.arange(GRID_W, dtype=jnp.int32), rows)
    return row, col


def _rope_1d(x, pos):
    half = x.shape[-1] // 2
    inv_freq = ROPE_THETA ** (-jnp.arange(half, dtype=jnp.float32) / half)
    ang = pos.astype(jnp.float32)[:, None] * inv_freq[None, :]
    cos = jnp.cos(ang)[:, None, :]
    sin = jnp.sin(ang)[:, None, :]
    xf = x.astype(jnp.float32)
    x1, x2 = xf[..., :half], xf[..., half:]
    return jnp.concatenate([x1 * cos - x2 * sin, x1 * sin + x2 * cos], axis=-1).astype(x.dtype)


def _rope_2d(x, row, col):
    d = x.shape[-1] // 2
    return jnp.concatenate([_rope_1d(x[..., :d], row), _rope_1d(x[..., d:], col)], axis=-1)


def _depthwise_conv(x, w, b):
    k, ch = w.shape
    y = lax.conv_general_dilated(x, w[:, None, :].astype(x.dtype), (1,), [((k - 1) // 2, k // 2)],
                                 dimension_numbers=('NWC', 'WIO', 'NWC'), feature_group_count=ch)
    return y + b


def _modulation(cond, w, b):
    m = (jax.nn.silu(cond) @ w + b)[..., None, :]
    return jnp.split(m, 6, axis=-1)


def _modulate(x, shift, scale):
    return x * (1.0 + scale) + shift


def _rglru_coeffs(u, wa, ba, wx, bx, lam):
    bsz, n, w = u.shape
    ub = u.reshape(bsz, n, LRU_BLOCKS, LRU_BLOCK_DIM)
    r = jax.nn.sigmoid((jnp.einsum('bnkc,kcd->bnkd', ub, wa).reshape(bsz, n, w) + ba).astype(jnp.float32))
    i = jax.nn.sigmoid((jnp.einsum('bnkc,kcd->bnkd', ub, wx).reshape(bsz, n, w) + bx).astype(jnp.float32))
    log_a = -LRU_C * r * jax.nn.softplus(-lam.astype(jnp.float32))
    a = jnp.exp(log_a)
    b = jnp.sqrt(-jnp.expm1(2.0 * log_a)) * (i * u.astype(jnp.float32))
    return a, b


def _linear_scan(a, b, h0):
    b = b.at[:, 0].add(a[:, 0] * h0)

    def combine(left, right):
        a_l, b_l = left
        a_r, b_r = right
        return a_l * a_r, a_r * b_l + b_r

    return lax.associative_scan(combine, (a, b), axis=1)[1]


def _rglru_bidir(u_ctx, u_lat, wa, ba, wx, bx, lam):
    h_ctx = 0.0
    h_lat = 0.0
    for d in range(2):
        a_c, b_c = _rglru_coeffs(u_ctx, wa[d], ba[d], wx[d], bx[d], lam[d])
        a_l, b_l = _rglru_coeffs(u_lat, wa[d], ba[d], wx[d], bx[d], lam[d])
        if d == 1:
            a_c, b_c, a_l, b_l = [jnp.flip(t, axis=1) for t in (a_c, b_c, a_l, b_l)]
        s_c = _linear_scan(a_c, b_c, jnp.zeros_like(a_c[:, 0]))
        s_l = _linear_scan(a_l, b_l, s_c[:, -1])
        if d == 1:
            s_c, s_l = jnp.flip(s_c, axis=1), jnp.flip(s_l, axis=1)
        h_ctx = h_ctx + s_c
        h_lat = h_lat + s_l
    return h_ctx, h_lat


def _sink_column(sink, kvh, g, shape):
    return jnp.broadcast_to(sink.astype(jnp.float32).reshape(kvh, g)[:, :, None, None], shape)


def _swa_latent(q, k, v, k_ctx, v_ctx, sink):
    bsz, n, h, hd = q.shape
    kvh = k.shape[2]
    g = h // kvh
    nb = n // Q_BLOCK
    qb = q.reshape(bsz, nb, Q_BLOCK, kvh, g, hd)

    def band(t):
        tb = jnp.pad(t.reshape(bsz, nb, Q_BLOCK, kvh, hd), ((0, 0), (1, 1), (0, 0), (0, 0), (0, 0)))
        return jnp.concatenate([tb[:, :-2], tb[:, 1:-1], tb[:, 2:]], axis=2)

    kb, vb = band(k), band(v)
    scale = hd ** -0.5
    s_win = jnp.einsum('bnqkgd,bnmkd->bnkgqm', qb, kb).astype(jnp.float32) * scale
    qpos = jnp.arange(nb)[:, None] * Q_BLOCK + jnp.arange(Q_BLOCK)[None, :]
    kpos = (jnp.arange(nb)[:, None] - 1) * Q_BLOCK + jnp.arange(3 * Q_BLOCK)[None, :]
    valid = (jnp.abs(qpos[:, :, None] - kpos[:, None, :]) <= WINDOW) & (kpos[:, None, :] >= 0) & (kpos[:, None, :] < n)
    s_win = jnp.where(valid[None, :, None, None], s_win, NEG_INF)
    s_ctx = jnp.einsum('bnqkgd,bckd->bnkgqc', qb, k_ctx).astype(jnp.float32) * scale
    s_sink = _sink_column(sink, kvh, g, s_win.shape[:-1] + (1,))
    p = jax.nn.softmax(jnp.concatenate([s_win, s_ctx, s_sink], axis=-1), axis=-1)
    nw = 3 * Q_BLOCK
    p_win = p[..., :nw].astype(v.dtype)
    p_ctx = p[..., nw:nw + k_ctx.shape[1]].astype(v.dtype)
    out = jnp.einsum('bnkgqm,bnmkd->bnqkgd', p_win, vb) + jnp.einsum('bnkgqc,bckd->bnqkgd', p_ctx, v_ctx)
    return out.reshape(bsz, n, h * hd)


def _gqa_sink_dense(q, k, v, sink):
    bsz, n, h, hd = q.shape
    kvh = k.shape[2]
    g = h // kvh
    qg = q.reshape(bsz, n, kvh, g, hd)
    s = jnp.einsum('bqkgd,bckd->bkgqc', qg, k).astype(jnp.float32) * hd ** -0.5
    s_sink = _sink_column(sink, kvh, g, s.shape[:-1] + (1,))
    p = jax.nn.softmax(jnp.concatenate([s, s_sink], axis=-1), axis=-1)[..., :-1]
    return jnp.einsum('bkgqc,bckd->bqkgd', p.astype(v.dtype), v).reshape(bsz, n, h * hd)


def _dense_attn(q, k, v):
    s = jnp.einsum('bqhd,bkhd->bhqk', q, k).astype(jnp.float32) * q.shape[-1] ** -0.5
    p = jax.nn.softmax(s, axis=-1).astype(v.dtype)
    return jnp.einsum('bhqk,bkhd->bqhd', p, v)


def _mla_q(cq, q_norm, w_uq, pos):
    bsz, n, _ = cq.shape
    q = (_rms_norm(cq, q_norm) @ w_uq).reshape(bsz, n, MLA_HEADS, MLA_NOPE + MLA_ROPE)
    q_pe = q[..., MLA_NOPE:]
    if pos is not None:
        q_pe = _rope_2d(q_pe, *pos)
    return jnp.concatenate([q[..., :MLA_NOPE], q_pe], axis=-1)


def _mla_kv(ckv, kpe, kv_norm, w_uk, w_uv, pos):
    bsz, n, _ = ckv.shape
    c_n = _rms_norm(ckv, kv_norm)
    k_nope = (c_n @ w_uk).reshape(bsz, n, MLA_HEADS, MLA_NOPE)
    v = (c_n @ w_uv).reshape(bsz, n, MLA_HEADS, MLA_V)
    k_pe = kpe[:, :, None, :]
    if pos is not None:
        k_pe = _rope_2d(k_pe, *pos)
    k = jnp.concatenate([k_nope, jnp.broadcast_to(k_pe, (bsz, n, MLA_HEADS, MLA_ROPE))], axis=-1)
    return k, v


def _mla_latent(q, k, v, k_ctx, v_ctx):
    bsz, n, h, dq = q.shape
    keys = jnp.concatenate([k_ctx, k], axis=1)
    vals = jnp.concatenate([v_ctx, v], axis=1)
    qb = jnp.moveaxis(q.reshape(bsz, n // Q_BLOCK, Q_BLOCK, h, dq), 1, 0)
    out = lax.map(lambda qi: _dense_attn(qi, keys, vals), qb)
    return jnp.moveaxis(out, 0, 1).reshape(bsz, n, h * v.shape[-1])


def _conformer_conv(u, dw_w, dw_b, ln_g, ln_b):
    a, gate = jnp.split(u, 2, axis=-1)
    y = _depthwise_conv(a * jax.nn.sigmoid(gate), dw_w, dw_b)
    return jax.nn.silu(_layer_norm(y, ln_g, ln_b))


def _even_mixer(h_lat, h_ctx, pos, need_ctx, w_in, b_in, conv_w, conv_b, wa, ba, wx, bx, lam, sink, w_out, b_out):
    splits = np.cumsum([LRU_WIDTH, LRU_WIDTH, SWA_HEADS * SWA_HEAD_DIM, SWA_KV_HEADS * SWA_HEAD_DIM]).tolist()

    def project(h):
        bsz, n, _ = h.shape
        g, u, q, k, v = jnp.split(h @ w_in + b_in, splits, axis=-1)
        return (g, u, q.reshape(bsz, n, SWA_HEADS, SWA_HEAD_DIM),
                k.reshape(bsz, n, SWA_KV_HEADS, SWA_HEAD_DIM), v.reshape(bsz, n, SWA_KV_HEADS, SWA_HEAD_DIM))

    g_l, u_l, q_l, k_l, v_l = project(h_lat)
    g_c, u_c, q_c, k_c, v_c = project(h_ctx)
    rec_c, rec_l = _rglru_bidir(_depthwise_conv(u_c, conv_w, conv_b), _depthwise_conv(u_l, conv_w, conv_b),
                                wa, ba, wx, bx, lam)
    att_l = _swa_latent(_rope_2d(q_l, *pos), _rope_2d(k_l, *pos), v_l, k_c, v_c, sink)
    y_l = jnp.concatenate([jax.nn.gelu(g_l) * rec_l.astype(g_l.dtype), att_l], axis=-1) @ w_out + b_out
    y_c = None
    if need_ctx:
        att_c = _gqa_sink_dense(q_c, k_c, v_c, sink)
        y_c = jnp.concatenate([jax.nn.gelu(g_c) * rec_c.astype(g_c.dtype), att_c], axis=-1) @ w_out + b_out
    return y_l, y_c


def _odd_mixer(h_lat, h_ctx, pos, need_ctx, w_in, b_in, q_norm, kv_norm, w_uq, w_uk, w_uv,
               dw_w, dw_b, cln_g, cln_b, w_out, b_out):
    splits = np.cumsum([MLA_Q_RANK, MLA_KV_RANK, MLA_ROPE]).tolist()
    cq_l, ckv_l, kpe_l, cv_l = jnp.split(h_lat @ w_in + b_in, splits, axis=-1)
    cq_c, ckv_c, kpe_c, cv_c = jnp.split(h_ctx @ w_in + b_in, splits, axis=-1)
    k_c, v_c = _mla_kv(ckv_c, kpe_c, kv_norm, w_uk, w_uv, None)
    k_l, v_l = _mla_kv(ckv_l, kpe_l, kv_norm, w_uk, w_uv, pos)
    att_l = _mla_latent(_mla_q(cq_l, q_norm, w_uq, pos), k_l, v_l, k_c, v_c)
    conv_l = _conformer_conv(cv_l, dw_w, dw_b, cln_g, cln_b)
    y_l = jnp.concatenate([att_l, conv_l], axis=-1) @ w_out + b_out
    y_c = None
    if need_ctx:
        bsz, n, _ = cq_c.shape
        att_c = _dense_attn(_mla_q(cq_c, q_norm, w_uq, None), k_c, v_c).reshape(bsz, n, MLA_HEADS * MLA_V)
        conv_c = _conformer_conv(cv_c, dw_w, dw_b, cln_g, cln_b)
        y_c = jnp.concatenate([att_c, conv_c], axis=-1) @ w_out + b_out
    return y_l, y_c


def _hier_moe(h, w_group, b_group, w_router, b_router, w1, w3, w2):
    n_tok, d = h.shape
    g_logits = (h @ w_group + b_group).astype(jnp.float32)
    g_idx = jnp.argmax(g_logits, axis=-1)
    g_prob = jnp.take_along_axis(jax.nn.softmax(g_logits, axis=-1), g_idx[:, None], axis=-1)
    e_logits = (h @ w_router + b_router).astype(jnp.float32).reshape(n_tok, N_GROUPS, EXPERTS_PER_GROUP)
    e_logits = jnp.take_along_axis(e_logits, g_idx[:, None, None], axis=1)[:, 0]
    top_p, top_i = lax.top_k(jax.nn.softmax(e_logits, axis=-1), TOP_K)
    gates = g_prob * top_p / jnp.sum(top_p, axis=-1, keepdims=True)
    expert = (g_idx[:, None] * EXPERTS_PER_GROUP + top_i).reshape(-1)
    token = jnp.repeat(jnp.arange(n_tok, dtype=jnp.int32), TOP_K)
    gate = gates.reshape(-1)
    n_assign = n_tok * TOP_K
    order = jnp.argsort(expert)
    e_s, t_s, g_s = expert[order], token[order], gate[order]
    counts = jnp.bincount(expert, length=N_EXPERTS)
    starts = jnp.cumsum(counts) - counts
    padded = (counts + MOE_BLOCK - 1) // MOE_BLOCK * MOE_BLOCK
    p_ends = jnp.cumsum(padded)
    p_starts = p_ends - padded
    dest = p_starts[e_s] + jnp.arange(n_assign) - starts[e_s]
    n_blocks = -(-(n_assign + N_EXPERTS * (MOE_BLOCK - 1)) // MOE_BLOCK)
    slot_tok = jnp.zeros((n_blocks * MOE_BLOCK,), jnp.int32).at[dest].set(t_s)
    slot_gate = jnp.zeros((n_blocks * MOE_BLOCK,), jnp.float32).at[dest].set(g_s)
    block_expert = jnp.minimum(jnp.searchsorted(p_ends, jnp.arange(n_blocks) * MOE_BLOCK, side='right'), N_EXPERTS - 1)
    xb = h[slot_tok].reshape(n_blocks, MOE_BLOCK, d)

    def expert_block(args):
        xi, e = args
        return (jax.nn.silu(xi @ w1[e]) * (xi @ w3[e])) @ w2[e]

    yb = lax.map(expert_block, (xb, block_expert)).reshape(-1, d)
    return jax.ops.segment_sum(yb * slot_gate[:, None].astype(yb.dtype), slot_tok, num_segments=n_tok)


def setup_inputs(seed: int = 0) -> dict:
    key = jax.random.key(seed)
    keys = jax.random.split(key, 64)
    counter = [0]

    def nxt():
        k = keys[counter[0]]
        counter[0] += 1
        return k

    def nrm(shape, scale):
        return jax.random.normal(nxt(), shape, jnp.float32) * scale

    def gain(shape):
        return 1.0 + nrm(shape, 0.05)

    D = D_MODEL
    u = jax.random.uniform(nxt(), (N_EVEN, 2, LRU_WIDTH), jnp.float32, 0.9, 0.999)
    a0 = u ** (1.0 / LRU_C)
    lam = jnp.log(a0) - jnp.log1p(-a0)
    return {
        "x": nrm((BATCH, SEQ, D), 1.0),
        "c": nrm((BATCH, D), 1.0),
        "ctx": nrm((BATCH, CTX_LEN, D), 1.0),
        "c_ctx": nrm((D,), 1.0),
        "w_mod": nrm((DEPTH, D, 6 * D), 0.5 * D ** -0.5),
        "b_mod": nrm((DEPTH, 6 * D), 0.02),
        "ln_g": gain((DEPTH, 2, D)),
        "ln_b": nrm((DEPTH, 2, D), 0.02),
        "e_w_in": nrm((N_EVEN, D, EVEN_IN), D ** -0.5),
        "e_b_in": nrm((N_EVEN, EVEN_IN), 0.02),
        "e_conv_w": nrm((N_EVEN, LRU_CONV_W, LRU_WIDTH), LRU_CONV_W ** -0.5),
        "e_conv_b": nrm((N_EVEN, LRU_WIDTH), 0.02),
        "e_lru_wa": nrm((N_EVEN, 2, LRU_BLOCKS, LRU_BLOCK_DIM, LRU_BLOCK_DIM), LRU_BLOCK_DIM ** -0.5),
        "e_lru_ba": nrm((N_EVEN, 2, LRU_WIDTH), 0.02),
        "e_lru_wx": nrm((N_EVEN, 2, LRU_BLOCKS, LRU_BLOCK_DIM, LRU_BLOCK_DIM), LRU_BLOCK_DIM ** -0.5),
        "e_lru_bx": nrm((N_EVEN, 2, LRU_WIDTH), 0.02),
        "e_lru_lambda": lam,
        "e_sink": nrm((N_EVEN, SWA_HEADS), 0.5),
        "e_w_out": nrm((N_EVEN, EVEN_MIX, D), EVEN_MIX ** -0.5 * DN_BETA),
        "e_b_out": nrm((N_EVEN, D), 0.02),
        "o_w_in": nrm((N_ODD, D, ODD_IN), D ** -0.5),
        "o_b_in": nrm((N_ODD, ODD_IN), 0.02),
        "o_q_norm": gain((N_ODD, MLA_Q_RANK)),
        "o_kv_norm": gain((N_ODD, MLA_KV_RANK)),
        "o_w_uq": nrm((N_ODD, MLA_Q_RANK, MLA_HEADS * (MLA_NOPE + MLA_ROPE)), MLA_Q_RANK ** -0.5),
        "o_w_uk": nrm((N_ODD, MLA_KV_RANK, MLA_HEADS * MLA_NOPE), MLA_KV_RANK ** -0.5),
        "o_w_uv": nrm((N_ODD, MLA_KV_RANK, MLA_HEADS * MLA_V), MLA_KV_RANK ** -0.5),
        "o_dw_w": nrm((N_ODD, CONF_K, CONF_CH), CONF_K ** -0.5),
        "o_dw_b": nrm((N_ODD, CONF_CH), 0.02),
        "o_cln_g": gain((N_ODD, CONF_CH)),
        "o_cln_b": nrm((N_ODD, CONF_CH), 0.02),
        "o_w_out": nrm((N_ODD, ODD_MIX, D), ODD_MIX ** -0.5 * DN_BETA),
        "o_b_out": nrm((N_ODD, D), 0.02),
        "moe_w_group": nrm((DEPTH, D, N_GROUPS), D ** -0.5),
        "moe_b_group": nrm((DEPTH, N_GROUPS), 0.01),
        "moe_w_router": nrm((DEPTH, D, N_EXPERTS), D ** -0.5),
        "moe_b_router": nrm((DEPTH, N_EXPERTS), 0.01),
        "moe_w1": nrm((DEPTH, N_EXPERTS, D, D_EXPERT), D ** -0.5),
        "moe_w3": nrm((DEPTH, N_EXPERTS, D, D_EXPERT), D ** -0.5),
        "moe_w2": nrm((DEPTH, N_EXPERTS, D_EXPERT, D), D_EXPERT ** -0.5 * DN_BETA),
    }


def reference(x, c, ctx, c_ctx, w_mod, b_mod, ln_g, ln_b,
              e_w_in, e_b_in, e_conv_w, e_conv_b, e_lru_wa, e_lru_ba, e_lru_wx, e_lru_bx, e_lru_lambda,
              e_sink, e_w_out, e_b_out,
              o_w_in, o_b_in, o_q_norm, o_kv_norm, o_w_uq, o_w_uk, o_w_uv, o_dw_w, o_dw_b, o_cln_g, o_cln_b,
              o_w_out, o_b_out,
              moe_w_group, moe_b_group, moe_w_router, moe_b_router, moe_w1, moe_w3, moe_w2):
    pos = _grid_positions(x.shape[1])
    x_lat, x_ctx = x, ctx
    for layer in range(DEPTH):
        need_ctx = layer < DEPTH - 1
        m_l = _modulation(c, w_mod[layer], b_mod[layer])
        m_c = _modulation(c_ctx, w_mod[layer], b_mod[layer])
        h_l = _modulate(x_lat, m_l[0], m_l[1])
        h_c = _modulate(x_ctx, m_c[0], m_c[1])
        p = layer // 2
        if layer % 2 == 0:
            y_l, y_c = _even_mixer(h_l, h_c, pos, need_ctx, e_w_in[p], e_b_in[p], e_conv_w[p], e_conv_b[p],
                                   e_lru_wa[p], e_lru_ba[p], e_lru_wx[p], e_lru_bx[p], e_lru_lambda[p],
                                   e_sink[p], e_w_out[p], e_b_out[p])
        else:
            y_l, y_c = _odd_mixer(h_l, h_c, pos, need_ctx, o_w_in[p], o_b_in[p], o_q_norm[p], o_kv_norm[p],
                                  o_w_uq[p], o_w_uk[p], o_w_uv[p], o_dw_w[p], o_dw_b[p], o_cln_g[p], o_cln_b[p],
                                  o_w_out[p], o_b_out[p])
        x_lat = _layer_norm(DN_ALPHA * x_lat + m_l[2] * y_l, ln_g[layer, 0], ln_b[layer, 0])
        f_l = _modulate(x_lat, m_l[3], m_l[4])
        n_lat = f_l.shape[0] * f_l.shape[1]
        if need_ctx:
            x_ctx = _layer_norm(DN_ALPHA * x_ctx + m_c[2] * y_c, ln_g[layer, 0], ln_b[layer, 0])
            f_c = _modulate(x_ctx, m_c[3], m_c[4])
            tokens = jnp.concatenate([f_l.reshape(n_lat, -1), f_c.reshape(-1, f_c.shape[-1])], axis=0)
        else:
            tokens = f_l.reshape(n_lat, -1)
        ffn = _hier_moe(tokens, moe_w_group[layer], moe_b_group[layer], moe_w_router[layer], moe_b_router[layer],
                        moe_w1[layer], moe_w3[layer], moe_w2[layer])
        x_lat = _layer_norm(DN_ALPHA * x_lat + m_l[5] * ffn[:n_lat].reshape(x_lat.shape), ln_g[layer, 1], ln_b[layer, 1])
        if need_ctx:
            x_ctx = _layer_norm(DN_ALPHA * x_ctx + m_c[5] * ffn[n_lat:].reshape(x_ctx.shape), ln_g[layer, 1], ln_b[layer, 1])
    return x_lat
```

```python
import functools

import numpy as np
import jax
import jax.numpy as jnp
from jax import lax
from jax.experimental import pallas as pl
from jax.experimental.pallas import tpu as pltpu

F32 = jnp.float32
BF16 = jnp.bfloat16

D_MODEL = 1024
DEPTH = 2
GRID_W = 64
ROPE_THETA = 10000.0
LN_EPS = 1e-6
NEG_INF = -1e30
DN_ALPHA = (2 * DEPTH) ** 0.25

LRU_WIDTH = 512
LRU_BLOCKS = 8
LRU_BLOCK_DIM = LRU_WIDTH // LRU_BLOCKS
LRU_CONV_W = 4
LRU_C = 8.0
SWA_HEADS = 8
SWA_KV_HEADS = 2
SWA_HEAD_DIM = 64
WINDOW = 128
MLA_HEADS = 8
MLA_Q_RANK = 256
MLA_KV_RANK = 128
MLA_NOPE = 64
MLA_ROPE = 32
MLA_V = 64
CONF_CH = 512
CONF_K = 31
N_GROUPS = 4
EXPERTS_PER_GROUP = 8
N_EXPERTS = N_GROUPS * EXPERTS_PER_GROUP
TOP_K = 2
D_EXPERT = 512

LANES = 128
SUBLANES = 8
ROW_TILE = 256
SCAN_SEG = ROW_TILE // SUBLANES
CONF_HALO = 16
CONF_SUB = 32
MOE_TILE = 256
VMEM_LIMIT = 48 * 1024 * 1024


def _cparams(n_axes, vmem=VMEM_LIMIT):
    return pltpu.CompilerParams(dimension_semantics=("arbitrary",) * n_axes, vmem_limit_bytes=vmem)


def _dot(a, b):
    return jnp.dot(a, b, preferred_element_type=F32)


def _dot_nt(a, b):
    return lax.dot_general(a, b, (((1,), (1,)), ((), ())), preferred_element_type=F32)


def _split_bf16(a):
    hi = a.astype(BF16)
    lo = (a - hi.astype(F32)).astype(BF16)
    return hi, lo


def _silu(x):
    return x * jax.nn.sigmoid(x)


def _gelu_tanh(x):
    return 0.5 * x * (1.0 + jnp.tanh(np.sqrt(2.0 / np.pi).astype(np.float32) * (x + 0.044715 * (x * x * x))))


def _layer_norm(x, g, b):
    mu = jnp.mean(x, axis=-1, keepdims=True)
    xc = x - mu
    var = jnp.mean(xc * xc, axis=-1, keepdims=True)
    return xc * lax.rsqrt(var + LN_EPS) * g + b


def _select_mod(mb_ref, mc_ref, idx, is_ctx):
    return jnp.where(is_ctx, mc_ref[0, idx:idx + 1, :], mb_ref[0, idx:idx + 1, :])


def _mod_kernel(c_ref, w_ref, b_ref, o_ref):
    c = c_ref[...]
    s_hi, s_lo = _split_bf16(_silu(c))
    w_hi, w_lo = _split_bf16(w_ref[0])
    o_ref[0] = _dot(s_hi, w_hi) + _dot(s_lo, w_hi) + _dot(s_hi, w_lo) + b_ref[0]


def _modulation(cond8, w_mod, b_mod):
    depth, d, n = w_mod.shape
    tn = 1536
    return pl.pallas_call(
        _mod_kernel,
        grid=(depth, n // tn),
        in_specs=[pl.BlockSpec((SUBLANES, d), lambda l, j: (0, 0)),
                  pl.BlockSpec((1, d, tn), lambda l, j: (l, 0, j)),
                  pl.BlockSpec((1, 1, tn), lambda l, j: (l, 0, j))],
        out_specs=pl.BlockSpec((1, SUBLANES, tn), lambda l, j: (l, 0, j)),
        out_shape=jax.ShapeDtypeStruct((depth, SUBLANES, n), F32),
        compiler_params=_cparams(2),
        name="modulation",
    )(cond8, w_mod, b_mod.reshape(depth, 1, n))


def _rope_pattern(width, row, col):
    d = width // 2
    half = d // 2
    inv = ROPE_THETA ** (-jnp.arange(half, dtype=F32) / half)
    ang_r = row.astype(F32)[:, None] * inv[None, :]
    ang_c = col.astype(F32)[:, None] * inv[None, :]
    cos = jnp.concatenate([jnp.cos(ang_r)] * 2 + [jnp.cos(ang_c)] * 2, axis=1)
    sin = jnp.concatenate([-jnp.sin(ang_r), jnp.sin(ang_r), -jnp.sin(ang_c), jnp.sin(ang_c)], axis=1)
    partner = np.concatenate([np.arange(half) + half, np.arange(half), d + np.arange(half) + half, d + np.arange(half)])
    return cos, sin, partner


def _in_even_kernel(x_ref, mb_ref, mc_ref, w_ref, b_ref, cos_ref, sin_ref,
                    g_ref, u_ref, q_ref, k_ref, v_ref, *, l_lat):
    i = pl.program_id(0)
    tm = x_ref.shape[1]
    rows = i * tm + lax.broadcasted_iota(jnp.int32, (tm, 1), 0)
    is_ctx = rows >= l_lat
    shift = _select_mod(mb_ref, mc_ref, 0, is_ctx)
    scale = _select_mod(mb_ref, mc_ref, 1, is_ctx)
    h = (x_ref[0] * (1.0 + scale) + shift).astype(BF16)

    def proj(lo, hi):
        return _dot(h, w_ref[:, lo:hi]) + b_ref[:, lo:hi]

    w = LRU_WIDTH
    g_ref[0] = proj(0, w)
    u_ref[0] = proj(w, 2 * w)
    cos = cos_ref[...]
    sin = sin_ref[...]
    cos4 = jnp.concatenate([cos] * 4, axis=1)
    sin4 = jnp.concatenate([sin] * 4, axis=1)
    o = 2 * w
    q = (proj(o, o + 512) * cos4 + proj(o + 512, o + 1024) * sin4) * (SWA_HEAD_DIM ** -0.5)
    q_ref[0] = q.astype(BF16)
    o += 1024
    cos2 = jnp.concatenate([cos] * 2, axis=1)
    sin2 = jnp.concatenate([sin] * 2, axis=1)
    k = proj(o, o + 256) * cos2 + proj(o + 256, o + 512) * sin2
    k_ref[0] = k.astype(BF16)
    o += 512
    v_ref[0] = proj(o, o + 256).astype(BF16)


def _in_even(x_all, mods, w, b, cos, sin, l_lat):
    bsz, s_all, d = x_all.shape
    tm = ROW_TILE
    nw = w.shape[1]
    row = lambda n: pl.BlockSpec((1, tm, n), lambda i, bb: (bb, i, 0))
    return pl.pallas_call(
        functools.partial(_in_even_kernel, l_lat=l_lat),
        grid=(s_all // tm, bsz),
        in_specs=[row(d),
                  pl.BlockSpec((1, 6, d), lambda i, bb: (bb, 0, 0)),
                  pl.BlockSpec((1, 6, d), lambda i, bb: (4, 0, 0)),
                  pl.BlockSpec((d, nw), lambda i, bb: (0, 0)),
                  pl.BlockSpec((1, nw), lambda i, bb: (0, 0)),
                  pl.BlockSpec((tm, LANES), lambda i, bb: (i, 0)),
                  pl.BlockSpec((tm, LANES), lambda i, bb: (i, 0))],
        out_specs=[row(512), row(512), row(512), row(256), row(256)],
        out_shape=[jax.ShapeDtypeStruct((bsz, s_all, 512), F32),
                   jax.ShapeDtypeStruct((bsz, s_all, 512), F32),
                   jax.ShapeDtypeStruct((bsz, s_all, 512), BF16),
                   jax.ShapeDtypeStruct((bsz, s_all, 256), BF16),
                   jax.ShapeDtypeStruct((bsz, s_all, 256), BF16)],
        compiler_params=_cparams(2),
        name="in_even",
    )(x_all, mods, mods, w, b, cos, sin)


def _softplus(x):
    return jnp.maximum(x, 0.0) + jnp.log1p(jnp.exp(-jnp.abs(x)))


def _lru_chunk_index(s, n_lat, reverse):
    if reverse:
        return jnp.where(s == 0, n_lat, n_lat - s)
    return jnp.where(s == 0, n_lat, s - 1)


def _lru_kernel(*refs, reverse, n_lat):
    if reverse:
        (u_ref, up_ref, un_ref, cw_ref, cb_ref, w_ref, bias_ref, lam_ref, g_ref, hf_ref,
         o_ref, ext, a_s, b_s, p_s, h_s, carry) = refs
    else:
        (u_ref, up_ref, un_ref, cw_ref, cb_ref, w_ref, bias_ref, lam_ref,
         o_ref, ext, a_s, b_s, p_s, h_s, carry) = refs
    s = pl.program_id(1)
    j = _lru_chunk_index(s, n_lat, reverse)
    t = ROW_TILE
    width = LRU_WIDTH

    @pl.when(s == 0)
    def _():
        carry[...] = jnp.zeros_like(carry)

    prev_ok = jnp.logical_and(j != 0, j != n_lat)
    next_ok = jnp.logical_and(j != n_lat - 1, j != n_lat)
    ext[0:SUBLANES, :] = jnp.where(prev_ok, up_ref[0], 0.0)
    ext[SUBLANES:SUBLANES + t, :] = u_ref[0]
    ext[SUBLANES + t:, :] = jnp.where(next_ok, un_ref[0], 0.0)
    uc = cb_ref[...]
    for kk in range(LRU_CONV_W):
        uc = uc + cw_ref[kk:kk + 1, :] * ext[SUBLANES - 1 + kk:SUBLANES - 1 + kk + t, :]

    z = _dot(uc.astype(BF16), w_ref[...]) + bias_ref[...]
    r = jax.nn.sigmoid(z[:, :width])
    ig = jax.nn.sigmoid(z[:, width:])
    log_a = (-LRU_C) * r * _softplus(-lam_ref[...])
    a = jnp.exp(log_a)
    b = jnp.sqrt(-jnp.tanh(log_a) * (a * a + 1.0)) * (ig * uc)

    steps = range(SCAN_SEG - 1, -1, -1) if reverse else range(SCAN_SEG)
    segs = range(SUBLANES - 1, -1, -1) if reverse else range(SUBLANES)
    for cb in range(width // LANES):
        cols = slice(cb * LANES, (cb + 1) * LANES)
        a_s[cb] = a[:, cols]
        b_s[cb] = b[:, cols]
        h = jnp.zeros((SUBLANES, LANES), F32)
        p = jnp.ones((SUBLANES, LANES), F32)
        for i in steps:
            sl = pl.ds(i, SUBLANES, stride=SCAN_SEG)
            ai = a_s[cb, sl, :]
            h = ai * h + b_s[cb, sl, :]
            p = p * ai
            h_s[cb, sl, :] = h
            p_s[cb, sl, :] = p
        c = carry[:, cols]
        seg_in = [None] * SUBLANES
        for sg in segs:
            seg_in[sg] = c
            c = p[sg:sg + 1, :] * c + h[sg:sg + 1, :]
        carry[:, cols] = c
        cin = jnp.concatenate(seg_in, axis=0)
        for i in range(SCAN_SEG):
            sl = pl.ds(i, SUBLANES, stride=SCAN_SEG)
            h_s[cb, sl, :] = h_s[cb, sl, :] + p_s[cb, sl, :] * cin
    hs = jnp.concatenate([h_s[cb] for cb in range(width // LANES)], axis=1)
    if reverse:
        o_ref[0] = (_gelu_tanh(g_ref[0]) * (hf_ref[0] + hs)).astype(o_ref.dtype)
    else:
        o_ref[0] = hs


def _lru(u, g, hf, conv_w, conv_b, w, bias, lam, reverse, l_lat):
    bsz, s_all, width = u.shape
    t = ROW_TILE
    n_chunks = s_all // t
    n_lat = l_lat // t
    hb = t // SUBLANES
    n_hb = s_all // SUBLANES
    cidx = functools.partial(_lru_chunk_index, n_lat=n_lat, reverse=reverse)
    chunk = pl.BlockSpec((1, t, width), lambda bb, s: (bb, cidx(s), 0))
    const = lambda shape: pl.BlockSpec(shape, lambda bb, s: (0,) * len(shape))
    in_specs = [chunk,
                pl.BlockSpec((1, SUBLANES, width), lambda bb, s: (bb, jnp.maximum(cidx(s) * hb - 1, 0), 0)),
                pl.BlockSpec((1, SUBLANES, width), lambda bb, s: (bb, jnp.minimum(cidx(s) * hb + hb, n_hb - 1), 0)),
                const((LRU_CONV_W, width)), const((1, width)), const((width, 2 * width)),
                const((1, 2 * width)), const((1, width))]
    args = [u, u, u, conv_w, conv_b, w, bias, lam]
    if reverse:
        in_specs += [chunk, chunk]
        args += [g, hf]
    return pl.pallas_call(
        functools.partial(_lru_kernel, reverse=reverse, n_lat=n_lat),
        grid=(bsz, n_chunks),
        in_specs=in_specs,
        out_specs=chunk,
        out_shape=jax.ShapeDtypeStruct((bsz, s_all, width), BF16 if reverse else F32),
        scratch_shapes=[pltpu.VMEM((t + 2 * SUBLANES, width), F32)]
                       + [pltpu.VMEM((width // LANES, t, LANES), F32)] * 4 + [pltpu.VMEM((1, width), F32)],
        compiler_params=_cparams(2),
        name="lru_bwd" if reverse else "lru_fwd",
    )(*args)


def _swa_kernel(sink_ref, q_ref, k_ref, v_ref, o_ref, *, l_lat, n_ctx):
    i = pl.program_id(1)
    tq = q_ref.shape[1]
    wk = tq + 2 * WINDOW
    is_lat = i < l_lat // tq
    ws = pl.multiple_of(jnp.clip(i * tq - WINDOW, 0, l_lat - wk), WINDOW)
    qpos = i * tq + lax.broadcasted_iota(jnp.int32, (tq, 1), 0)
    kpos = ws + lax.broadcasted_iota(jnp.int32, (1, wk), 1)
    band = jnp.logical_and(jnp.abs(qpos - kpos) <= WINDOW, is_lat)
    lo = lax.broadcasted_iota(jnp.int32, (1, LANES), 1) < SWA_HEAD_DIM
    heads_per_kv = SWA_HEADS // SWA_KV_HEADS
    for kh in range(SWA_KV_HEADS):
        cols = slice(LANES * kh, LANES * (kh + 1))
        kw = k_ref[0, pl.ds(ws, wk), cols]
        vw = v_ref[0, pl.ds(ws, wk), cols]
        kc = k_ref[0, l_lat:l_lat + n_ctx, cols]
        vc = v_ref[0, l_lat:l_lat + n_ctx, cols]
        for jq in range(heads_per_kv // 2):
            blk = kh * (heads_per_kv // 2) + jq
            qb = q_ref[0, :, LANES * blk:LANES * (blk + 1)]
            halves = []
            for half in range(2):
                qm = jnp.where(lo if half == 0 else jnp.logical_not(lo), qb, jnp.zeros_like(qb))
                sw = jnp.where(band, _dot_nt(qm, kw), NEG_INF)
                sc = _dot_nt(qm, kc)
                sink = sink_ref[2 * blk + half]
                m = jnp.maximum(jnp.maximum(jnp.max(sw, axis=-1, keepdims=True),
                                            jnp.max(sc, axis=-1, keepdims=True)), sink)
                pw = jnp.exp(sw - m)
                pc = jnp.exp(sc - m)
                den = (jnp.sum(pw, axis=-1, keepdims=True) + jnp.sum(pc, axis=-1, keepdims=True)
                       + jnp.exp(sink - m))
                halves.append((_dot(pw.astype(BF16), vw) + _dot(pc.astype(BF16), vc)) / den)
            o_ref[0, :, LANES * blk:LANES * (blk + 1)] = jnp.where(lo, halves[0], halves[1]).astype(o_ref.dtype)


def _swa(q, k2, v2, sink, l_lat):
    bsz, s_all, _ = q.shape
    tq = ROW_TILE
    n_ctx = s_all - l_lat
    return pl.pallas_call(
        functools.partial(_swa_kernel, l_lat=l_lat, n_ctx=n_ctx),
        grid_spec=pltpu.PrefetchScalarGridSpec(
            num_scalar_prefetch=1,
            grid=(bsz, s_all // tq),
            in_specs=[pl.BlockSpec((1, tq, 512), lambda bb, i, s: (bb, i, 0)),
                      pl.BlockSpec((1, s_all, 256), lambda bb, i, s: (bb, 0, 0)),
                      pl.BlockSpec((1, s_all, 256), lambda bb, i, s: (bb, 0, 0))],
            out_specs=pl.BlockSpec((1, tq, 512), lambda bb, i, s: (bb, i, 0))),
        out_shape=jax.ShapeDtypeStruct((bsz, s_all, 512), BF16),
        compiler_params=_cparams(2),
        name="swa",
    )(sink, q, k2, v2)


def _route(logits):
    lane = lax.broadcasted_iota(jnp.int32, logits.shape, 1)
    big = jnp.int32(2 ** 30)

    def first_argmax(vals, mask):
        v = jnp.where(mask, vals, NEG_INF)
        m = jnp.max(v, axis=-1, keepdims=True)
        idx = jnp.min(jnp.where(jnp.logical_and(mask, v == m), lane, big), axis=-1, keepdims=True)
        return m, idx

    gmask = lane < N_GROUPS
    gm, gidx = first_argmax(logits, gmask)
    g_prob = 1.0 / jnp.sum(jnp.where(gmask, jnp.exp(logits - gm), 0.0), axis=-1, keepdims=True)
    base = N_GROUPS + gidx * EXPERTS_PER_GROUP
    emask = jnp.logical_and(lane >= base, lane < base + EXPERTS_PER_GROUP)
    m0, i0 = first_argmax(logits, emask)
    m1, i1 = first_argmax(logits, jnp.logical_and(emask, lane != i0))
    w1 = jnp.exp(m1 - m0)
    inv = g_prob / (1.0 + w1)
    return i0 - N_GROUPS, i1 - N_GROUPS, inv, w1 * inv


def _post_mix_kernel(a_ref, b_ref, x_ref, mb_ref, mc_ref, w_ref, bo_ref, lng_ref, lnb_ref, wr_ref, br_ref,
                     x1_ref, f_ref, r_ref, *, l_lat):
    i = pl.program_id(1)
    tm = x_ref.shape[1]
    rows = i * tm + lax.broadcasted_iota(jnp.int32, (tm, 1), 0)
    is_ctx = rows >= l_lat
    half = a_ref.shape[2]
    y = _dot(a_ref[0], w_ref[0:half, :]) + _dot(b_ref[0], w_ref[half:, :]) + bo_ref[...]
    gate = _select_mod(mb_ref, mc_ref, 2, is_ctx)
    x1 = _layer_norm(DN_ALPHA * x_ref[0] + gate * y, lng_ref[...], lnb_ref[...])
    x1_ref[0] = x1
    f = x1 * (1.0 + _select_mod(mb_ref, mc_ref, 4, is_ctx)) + _select_mod(mb_ref, mc_ref, 3, is_ctx)
    f_ref[0] = f.astype(BF16)
    f_hi, f_lo = _split_bf16(f)
    two = _dot(f_hi, wr_ref[...])
    logits = two[:, :LANES] + two[:, LANES:] + _dot(f_lo, wr_ref[:, :LANES]) + br_ref[...]
    e0, e1, g0, g1 = _route(logits)
    lane = lax.broadcasted_iota(jnp.int32, (tm, LANES), 1)
    r_ref[0] = jnp.where(lane == 0, e0.astype(F32),
                         jnp.where(lane == 1, e1.astype(F32), jnp.where(lane == 2, g0, jnp.where(lane == 3, g1, 0.0))))


def _post_mix(a, b, x, mods, w_out, b_out, ln_g, ln_b, w_r, b_r, n_rows, l_lat):
    bsz = x.shape[0]
    d = x.shape[2]
    tm = ROW_TILE
    half = a.shape[2]
    row = lambda n: pl.BlockSpec((1, tm, n), lambda bb, i: (bb, i, 0))
    const = lambda shape: pl.BlockSpec(shape, lambda bb, i: (0,) * len(shape))
    return pl.pallas_call(
        functools.partial(_post_mix_kernel, l_lat=l_lat),
        grid=(bsz, n_rows // tm),
        in_specs=[row(half), row(half), row(d),
                  pl.BlockSpec((1, 6, d), lambda bb, i: (bb, 0, 0)),
                  pl.BlockSpec((1, 6, d), lambda bb, i: (4, 0, 0)),
                  const((2 * half, d)), const((1, d)), const((1, d)), const((1, d)),
                  const((d, 2 * LANES)), const((1, LANES))],
        out_specs=[row(d), row(d), row(LANES)],
        out_shape=[jax.ShapeDtypeStruct((bsz, n_rows, d), F32),
                   jax.ShapeDtypeStruct((bsz, n_rows, d), BF16),
                   jax.ShapeDtypeStruct((bsz, n_rows, LANES), F32)],
        compiler_params=_cparams(2),
        name="post_mix",
    )(a, b, x, mods, mods, w_out, b_out, ln_g, ln_b, w_r, b_r)


def _expert_kernel(be_ref, nb_ref, x_ref, gate_ref, w1_ref, w3_ref, w2_ref, o_ref):
    i = pl.program_id(0)

    @pl.when(i < nb_ref[0])
    def _():
        xb = x_ref[...]
        h1 = _dot(xb, w1_ref[0])
        h3 = _dot(xb, w3_ref[0])
        act = (_silu(h1) * h3).astype(BF16)
        o_ref[...] = _dot(act, w2_ref[0]) * gate_ref[...]

    @pl.when(i >= nb_ref[0])
    def _():
        o_ref[...] = jnp.zeros_like(o_ref)


def _experts(block_expert, n_used, xb, slot_gate, w1, w3, w2):
    n_slots, d = xb.shape
    tb = MOE_TILE
    de = w1.shape[2]
    return pl.pallas_call(
        _expert_kernel,
        grid_spec=pltpu.PrefetchScalarGridSpec(
            num_scalar_prefetch=2,
            grid=(n_slots // tb,),
            in_specs=[pl.BlockSpec((tb, d), lambda i, be, nb: (i, 0)),
                      pl.BlockSpec((tb, 1), lambda i, be, nb: (i, 0)),
                      pl.BlockSpec((1, d, de), lambda i, be, nb: (be[i], 0, 0)),
                      pl.BlockSpec((1, d, de), lambda i, be, nb: (be[i], 0, 0)),
                      pl.BlockSpec((1, de, d), lambda i, be, nb: (be[i], 0, 0))],
            out_specs=pl.BlockSpec((tb, d), lambda i, be, nb: (i, 0))),
        out_shape=jax.ShapeDtypeStruct((n_slots, d), F32),
        compiler_params=_cparams(1),
        name="experts",
    )(block_expert, n_used, xb, slot_gate, w1, w3, w2)


def _post_ffn_kernel(x_ref, y0_ref, y1_ref, mb_ref, mc_ref, lng_ref, lnb_ref, o_ref, *, l_lat):
    i = pl.program_id(1)
    tm = x_ref.shape[1]
    rows = i * tm + lax.broadcasted_iota(jnp.int32, (tm, 1), 0)
    is_ctx = rows >= l_lat
    gate = _select_mod(mb_ref, mc_ref, 5, is_ctx)
    ffn = y0_ref[0] + y1_ref[0]
    o_ref[0] = _layer_norm(DN_ALPHA * x_ref[0] + gate * ffn, lng_ref[...], lnb_ref[...])


def _post_ffn(x1, y0, y1, mods, ln_g, ln_b, l_lat):
    bsz, n_rows, d = x1.shape
    tm = ROW_TILE
    row = pl.BlockSpec((1, tm, d), lambda bb, i: (bb, i, 0))
    const = lambda shape: pl.BlockSpec(shape, lambda bb, i: (0,) * len(shape))
    return pl.pallas_call(
        functools.partial(_post_ffn_kernel, l_lat=l_lat),
        grid=(bsz, n_rows // tm),
        in_specs=[row, row, row,
                  pl.BlockSpec((1, 6, d), lambda bb, i: (bb, 0, 0)),
                  pl.BlockSpec((1, 6, d), lambda bb, i: (4, 0, 0)),
                  const((1, d)), const((1, d))],
        out_specs=row,
        out_shape=jax.ShapeDtypeStruct((bsz, n_rows, d), F32),
        compiler_params=_cparams(2),
        name="post_ffn",
    )(x1, y0, y1, mods, mods, ln_g, ln_b)


def _moe(f, route, w1, w3, w2):
    bsz, n_rows, d = f.shape
    n_tok = bsz * n_rows
    tb = MOE_TILE
    tokens = f.reshape(n_tok, d)
    r = route.reshape(n_tok, LANES)
    expert = r[:, 0:TOP_K].astype(jnp.int32).reshape(-1)
    gate = r[:, TOP_K:2 * TOP_K].reshape(-1)
    n_assign = n_tok * TOP_K
    token = jnp.repeat(jnp.arange(n_tok, dtype=jnp.int32), TOP_K)
    onehot = (expert[:, None] == jnp.arange(N_EXPERTS, dtype=jnp.int32)[None, :]).astype(jnp.int32)
    csum = jnp.cumsum(onehot, axis=0)
    rank = jnp.take_along_axis(csum, expert[:, None], axis=1)[:, 0] - 1
    counts = csum[-1]
    padded = (counts + tb - 1) // tb * tb
    p_ends = jnp.cumsum(padded)
    p_starts = p_ends - padded
    dest = p_starts[expert] + rank
    n_blocks = -(-(n_assign + N_EXPERTS * (tb - 1)) // tb)
    n_slots = n_blocks * tb
    slot_tok = jnp.zeros((n_slots,), jnp.int32).at[dest].set(token)
    slot_gate = jnp.zeros((n_slots,), F32).at[dest].set(gate)
    block_expert = jnp.minimum(jnp.searchsorted(p_ends, jnp.arange(n_blocks, dtype=jnp.int32) * tb, side='right'),
                               N_EXPERTS - 1).astype(jnp.int32)
    n_used = (p_ends[-1] // tb).astype(jnp.int32).reshape(1)
    xb = tokens[slot_tok]
    yb = _experts(block_expert, n_used, xb, slot_gate[:, None], w1, w3, w2)
    y = yb[dest].reshape(n_tok, TOP_K, d)
    return y[:, 0].reshape(bsz, n_rows, d), y[:, 1].reshape(bsz, n_rows, d)


def _rms(x, g):
    return x * lax.rsqrt(jnp.mean(x * x, axis=-1, keepdims=True) + LN_EPS) * g

def _in_odd_kernel(x_ref, mb_ref, mc_ref, w_ref, b_ref, qn_ref, kvn_ref, wuq_ref, wuk_ref, place_ref, wuvt_ref,
                   cq_ref, sq_ref, ck_ref, sk_ref, q_ref, k_ref, vt_ref, glu_ref, *, l_lat):
    i = pl.program_id(0)
    tm = x_ref.shape[1]
    rows = i * tm + lax.broadcasted_iota(jnp.int32, (tm, 1), 0)
    is_ctx = rows >= l_lat
    shift = _select_mod(mb_ref, mc_ref, 0, is_ctx)
    scale = _select_mod(mb_ref, mc_ref, 1, is_ctx)
    h = (x_ref[0] * (1.0 + scale) + shift).astype(BF16)

    def proj(lo, hi):
        return _dot(h, w_ref[:, lo:hi]) + b_ref[:, lo:hi]

    cqn = _rms(proj(0, MLA_Q_RANK), qn_ref[...]).astype(BF16)
    hw = MLA_HEADS * LANES
    cos8 = jnp.concatenate([cq_ref[...]] * MLA_HEADS, axis=1)
    sin8 = jnp.concatenate([sq_ref[...]] * MLA_HEADS, axis=1)
    q = _dot(cqn, wuq_ref[:, :hw]) * cos8 + _dot(cqn, wuq_ref[:, hw:]) * sin8
    q_ref[0] = (q * ((MLA_NOPE + MLA_ROPE) ** -0.5)).astype(BF16)
    o = MLA_Q_RANK
    cn = _rms(proj(o, o + MLA_KV_RANK), kvn_ref[...]).astype(BF16)
    o += MLA_KV_RANK
    kpe = proj(o, o + LANES) * ck_ref[...] + proj(o + LANES, o + 2 * LANES) * sk_ref[...]
    k_ref[0] = (_dot(cn, wuk_ref[...]) + _dot(kpe.astype(BF16), place_ref[...])).astype(BF16)
    vt = _dot_nt(wuvt_ref[...], cn)
    vt_ref[0] = vt.reshape(MLA_HEADS, 1, MLA_V, tm).astype(BF16)
    o += 2 * LANES
    glu_ref[0] = proj(o, o + CONF_CH) * jax.nn.sigmoid(proj(o + CONF_CH, o + 2 * CONF_CH))


def _in_odd(x_all, mods, w, b, qn, kvn, wuq, wuk, place, wuvt, cq, sq, ck, sk, l_lat, tm):
    bsz, s_all, d = x_all.shape
    nw = w.shape[1]
    nk = s_all // tm
    hw = MLA_HEADS * LANES
    row = lambda n: pl.BlockSpec((1, tm, n), lambda i, bb: (bb, i, 0))
    const = lambda shape: pl.BlockSpec(shape, lambda i, bb: (0,) * len(shape))
    tab = pl.BlockSpec((tm, LANES), lambda i, bb: (i, 0))
    return pl.pallas_call(
        functools.partial(_in_odd_kernel, l_lat=l_lat),
        grid=(nk, bsz),
        in_specs=[row(d),
                  pl.BlockSpec((1, 6, d), lambda i, bb: (bb, 0, 0)),
                  pl.BlockSpec((1, 6, d), lambda i, bb: (4, 0, 0)),
                  const((d, nw)), const((1, nw)), const((1, MLA_Q_RANK)), const((1, MLA_KV_RANK)),
                  const((MLA_Q_RANK, 2 * hw)), const((MLA_KV_RANK, hw)), const((LANES, hw)),
                  const((MLA_HEADS * MLA_V, MLA_KV_RANK)), tab, tab, tab, tab],
        out_specs=[row(hw), row(hw),
                   pl.BlockSpec((1, MLA_HEADS, 1, MLA_V, tm), lambda i, bb: (bb, 0, i, 0, 0)),
                   row(CONF_CH)],
        out_shape=[jax.ShapeDtypeStruct((bsz, s_all, hw), BF16),
                   jax.ShapeDtypeStruct((bsz, s_all, hw), BF16),
                   jax.ShapeDtypeStruct((bsz, MLA_HEADS, nk, MLA_V, tm), BF16),
                   jax.ShapeDtypeStruct((bsz, s_all, CONF_CH), F32)],
        compiler_params=_cparams(2),
        name="in_odd",
    )(x_all, mods, mods, w, b, qn, kvn, wuq, wuk, place, wuvt, cq, sq, ck, sk)


def _flash_kernel(q_ref, k_ref, vt_ref, o_ref, *, tk, nk):
    tq = q_ref.shape[1]
    accs = []
    for hh in range(2):
        q = q_ref[0, :, LANES * hh:LANES * (hh + 1)]

        def body(c, carry, q=q, hh=hh):
            m, l, acc = carry
            off = pl.multiple_of(c * tk, tk)
            k = k_ref[0, pl.ds(off, tk), LANES * hh:LANES * (hh + 1)]
            st = _dot_nt(k, q)
            m_new = jnp.maximum(m, jnp.max(st, axis=0, keepdims=True))
            alpha = jnp.exp(m - m_new)
            p = jnp.exp(st - m_new)
            l = alpha * l + jnp.sum(p, axis=0, keepdims=True)
            acc = alpha * acc + _dot(vt_ref[0, hh, c], p.astype(BF16))
            return m_new, l, acc

        m, l, acc = lax.fori_loop(0, nk, body, (jnp.full((1, tq), NEG_INF, F32), jnp.zeros((1, tq), F32),
                                                jnp.zeros((MLA_V, tq), F32)))
        accs.append(acc / l)
    o_ref[0] = jnp.concatenate(accs, axis=0).T.astype(o_ref.dtype)


def _flash(q, k, vt, l_lat, tk):
    bsz, s_all, _ = q.shape
    nk = s_all // tk
    tq = 256
    return pl.pallas_call(
        functools.partial(_flash_kernel, tk=tk, nk=nk),
        grid=(bsz, MLA_HEADS // 2, l_lat // tq),
        in_specs=[pl.BlockSpec((1, tq, 2 * LANES), lambda bb, hp, i: (bb, i, hp)),
                  pl.BlockSpec((1, s_all, 2 * LANES), lambda bb, hp, i: (bb, 0, hp)),
                  pl.BlockSpec((1, 2, nk, MLA_V, tk), lambda bb, hp, i: (bb, hp, 0, 0, 0))],
        out_specs=pl.BlockSpec((1, tq, LANES), lambda bb, hp, i: (bb, i, hp)),
        out_shape=jax.ShapeDtypeStruct((bsz, l_lat, MLA_HEADS * MLA_V), BF16),
        compiler_params=_cparams(3),
        name="mla_attention",
    )(q, k, vt)


def _conf_kernel(x_ref, xp_ref, xn_ref, w_ref, b_ref, g_ref, bb_ref, o_ref, ext):
    i = pl.program_id(1)
    n = pl.num_programs(1)
    t = x_ref.shape[1]
    hl = CONF_HALO
    ext[0:hl, :] = jnp.where(i > 0, xp_ref[0], 0.0)
    ext[hl:hl + t, :] = x_ref[0]
    ext[hl + t:, :] = jnp.where(i < n - 1, xn_ref[0], 0.0)
    off = hl - CONF_K // 2
    for r in range(t // CONF_SUB):
        acc = jnp.zeros((CONF_SUB, CONF_CH), F32) + b_ref[...]
        for kk in range(CONF_K):
            st = off + kk + r * CONF_SUB
            acc = acc + w_ref[kk:kk + 1, :] * ext[st:st + CONF_SUB, :]
        o_ref[0, r * CONF_SUB:(r + 1) * CONF_SUB, :] = _silu(_layer_norm(acc, g_ref[...], bb_ref[...])).astype(o_ref.dtype)


def _conformer(glu, w, b, g, beta, l_lat):
    bsz, _, ch = glu.shape
    t = ROW_TILE
    hl = CONF_HALO
    hb = t // hl
    n_hb = l_lat // hl
    const = lambda shape: pl.BlockSpec(shape, lambda bb, i: (0,) * len(shape))
    return pl.pallas_call(
        _conf_kernel,
        grid=(bsz, l_lat // t),
        in_specs=[pl.BlockSpec((1, t, ch), lambda bb, i: (bb, i, 0)),
                  pl.BlockSpec((1, hl, ch), lambda bb, i: (bb, jnp.maximum(i * hb - 1, 0), 0)),
                  pl.BlockSpec((1, hl, ch), lambda bb, i: (bb, jnp.minimum(i * hb + hb, n_hb - 1), 0)),
                  const((CONF_K, ch)), const((1, ch)), const((1, ch)), const((1, ch))],
        out_specs=pl.BlockSpec((1, t, ch), lambda bb, i: (bb, i, 0)),
        out_shape=jax.ShapeDtypeStruct((bsz, l_lat, ch), BF16),
        scratch_shapes=[pltpu.VMEM((t + 2 * hl, ch), F32)],
        compiler_params=_cparams(2),
        name="conformer_conv",
    )(glu, glu, glu, w, b, g, beta)


def _block_diag(w):
    nb, c, d = w.shape
    out = jnp.zeros((nb, c, nb, d), w.dtype)
    out = out.at[jnp.arange(nb), :, jnp.arange(nb), :].set(w)
    return out.reshape(nb * c, nb * d)


def _router_weights(w_group, b_group, w_router, b_router):
    d = w_group.shape[0]
    w = jnp.zeros((d, LANES), F32).at[:, :N_GROUPS].set(w_group).at[:, N_GROUPS:N_GROUPS + N_EXPERTS].set(w_router)
    hi, lo = _split_bf16(w)
    b = jnp.zeros((1, LANES), F32).at[0, :N_GROUPS].set(b_group).at[0, N_GROUPS:N_GROUPS + N_EXPERTS].set(b_router)
    return jnp.concatenate([hi, lo], axis=1), b


def kernel(x, c, ctx, c_ctx, w_mod, b_mod, ln_g, ln_b, e_w_in, e_b_in, e_conv_w, e_conv_b, e_lru_wa, e_lru_ba, e_lru_wx, e_lru_bx, e_lru_lambda, e_sink, e_w_out, e_b_out, o_w_in, o_b_in, o_q_norm, o_kv_norm, o_w_uq, o_w_uk, o_w_uv, o_dw_w, o_dw_b, o_cln_g, o_cln_b, o_w_out, o_b_out, moe_w_group, moe_b_group, moe_w_router, moe_b_router, moe_w1, moe_w3, moe_w2):
    bsz, l_lat, d = x.shape
    n_ctx = ctx.shape[1]
    s_all = l_lat + n_ctx
    assert d == D_MODEL and bsz <= 4 and l_lat % ROW_TILE == 0 and n_ctx == ROW_TILE and l_lat % GRID_W == 0
    assert w_mod.shape[0] == DEPTH == 2

    x_all = jnp.concatenate([x, ctx], axis=1)
    cond8 = jnp.zeros((SUBLANES, d), F32).at[:bsz].set(c).at[4].set(c_ctx)
    mods = _modulation(cond8, w_mod, b_mod).reshape(DEPTH, SUBLANES, 6, d)

    t_idx = jnp.arange(l_lat, dtype=jnp.int32)
    row_pos, col_pos = t_idx // GRID_W, t_idx % GRID_W

    def tables(width, lane_off):
        cos, sin, partner = _rope_pattern(width, row_pos, col_pos)
        cos_t = jnp.zeros((s_all, LANES), F32).at[:, :].set(1.0)
        cos_t = cos_t.at[:l_lat, lane_off:lane_off + width].set(cos)
        sin_t = jnp.zeros((s_all, LANES), F32).at[:l_lat, lane_off:lane_off + width].set(sin)
        return cos_t, sin_t, partner

    cos64, sin64, partner64 = _rope_pattern(SWA_HEAD_DIM, row_pos, col_pos)
    cos_e = jnp.ones((s_all, LANES), F32).at[:l_lat].set(jnp.concatenate([cos64] * 2, axis=1))
    sin_e = jnp.zeros((s_all, LANES), F32).at[:l_lat].set(jnp.concatenate([sin64] * 2, axis=1))
    w_in, b_in = e_w_in[0], e_b_in[0]
    oq = 2 * LRU_WIDTH
    ok = oq + SWA_HEADS * SWA_HEAD_DIM
    ov = ok + SWA_KV_HEADS * SWA_HEAD_DIM
    q_cols = oq + np.arange(SWA_HEADS * SWA_HEAD_DIM)
    qp_cols = oq + (np.arange(SWA_HEADS)[:, None] * SWA_HEAD_DIM + partner64[None, :]).reshape(-1)
    dup = np.repeat(np.arange(SWA_KV_HEADS), 2)
    k_cols = ok + (dup[:, None] * SWA_HEAD_DIM + np.arange(SWA_HEAD_DIM)[None, :]).reshape(-1)
    kp_cols = ok + (dup[:, None] * SWA_HEAD_DIM + partner64[None, :]).reshape(-1)
    v_cols = ov + (dup[:, None] * SWA_HEAD_DIM + np.arange(SWA_HEAD_DIM)[None, :]).reshape(-1)
    cols = np.concatenate([np.arange(oq), q_cols, qp_cols, k_cols, kp_cols, v_cols])
    g0, u0, q0, k0, v0 = _in_even(x_all, mods[0], w_in[:, cols].astype(BF16), b_in[cols][None, :], cos_e, sin_e, l_lat)

    def lru_weights(dd):
        w = jnp.concatenate([_block_diag(e_lru_wa[0, dd]), _block_diag(e_lru_wx[0, dd])], axis=1).astype(BF16)
        bias = jnp.concatenate([e_lru_ba[0, dd], e_lru_bx[0, dd]])[None, :]
        return w, bias, e_lru_lambda[0, dd][None, :]

    cw, cb = e_conv_w[0], e_conv_b[0][None, :]
    hf = _lru(u0, None, None, cw, cb, *lru_weights(0), reverse=False, l_lat=l_lat)
    rec = _lru(u0, g0, hf, cw, cb, *lru_weights(1), reverse=True, l_lat=l_lat)
    att = _swa(q0, k0, v0, e_sink[0], l_lat)
    w_r, b_r = _router_weights(moe_w_group[0], moe_b_group[0], moe_w_router[0], moe_b_router[0])
    x1, f, route = _post_mix(rec, att, x_all, mods[0], e_w_out[0].astype(BF16), e_b_out[0][None, :],
                             ln_g[0, 0][None, :], ln_b[0, 0][None, :], w_r, b_r, s_all, l_lat)
    y0, y1 = _moe(f, route, moe_w1[0].astype(BF16), moe_w3[0].astype(BF16), moe_w2[0].astype(BF16))
    x2 = _post_ffn(x1, y0, y1, mods[0], ln_g[0, 1][None, :], ln_b[0, 1][None, :], l_lat)

    tk = 768 if s_all % 768 == 0 else ROW_TILE
    cos32, sin32, partner32 = _rope_pattern(MLA_ROPE, row_pos, col_pos)
    w_in, b_in = o_w_in[0], o_b_in[0]
    o_pe = MLA_Q_RANK + MLA_KV_RANK
    o_cv = o_pe + MLA_ROPE

    def pad_cols(wm, bv, n):
        return (jnp.zeros((wm.shape[0], n), F32).at[:, :wm.shape[1]].set(wm),
                jnp.zeros((n,), F32).at[:bv.shape[0]].set(bv))

    pe_cols = o_pe + np.arange(MLA_ROPE)
    wa_, ba_ = pad_cols(w_in[:, pe_cols], b_in[pe_cols], LANES)
    wb_, bb_ = pad_cols(w_in[:, o_pe + partner32], b_in[o_pe + partner32], LANES)
    w_odd = jnp.concatenate([w_in[:, :o_pe], wa_, wb_, w_in[:, o_cv:]], axis=1).astype(BF16)
    b_odd = jnp.concatenate([b_in[:o_pe], ba_, bb_, b_in[o_cv:]])[None, :]
    ck = jnp.ones((s_all, LANES), F32).at[:l_lat, :MLA_ROPE].set(cos32)
    sk = jnp.zeros((s_all, LANES), F32).at[:l_lat, :MLA_ROPE].set(sin32)
    cq = jnp.ones((s_all, LANES), F32).at[:l_lat, MLA_NOPE:MLA_NOPE + MLA_ROPE].set(cos32)
    sq = jnp.zeros((s_all, LANES), F32).at[:l_lat, MLA_NOPE:MLA_NOPE + MLA_ROPE].set(sin32)
    hd = MLA_NOPE + MLA_ROPE
    wuq = o_w_uq[0].reshape(MLA_Q_RANK, MLA_HEADS, hd)
    wuq_main = jnp.zeros((MLA_Q_RANK, MLA_HEADS, LANES), F32).at[:, :, :hd].set(wuq)
    wuq_part = jnp.zeros((MLA_Q_RANK, MLA_HEADS, LANES), F32).at[:, :, MLA_NOPE:hd].set(wuq[:, :, MLA_NOPE + partner32])
    hw = MLA_HEADS * LANES
    wuq_ext = jnp.concatenate([wuq_main.reshape(MLA_Q_RANK, hw), wuq_part.reshape(MLA_Q_RANK, hw)], axis=1).astype(BF16)
    wuk = jnp.zeros((MLA_KV_RANK, MLA_HEADS, LANES), F32).at[:, :, :MLA_NOPE].set(
        o_w_uk[0].reshape(MLA_KV_RANK, MLA_HEADS, MLA_NOPE)).reshape(MLA_KV_RANK, hw).astype(BF16)
    place = np.zeros((LANES, MLA_HEADS, LANES), np.float32)
    for hh in range(MLA_HEADS):
        place[np.arange(MLA_ROPE), hh, MLA_NOPE + np.arange(MLA_ROPE)] = 1.0
    place = jnp.asarray(place.reshape(LANES, hw), BF16)
    wuvt = o_w_uv[0].T.astype(BF16)
    q1, k1, vt1, glu = _in_odd(x2, mods[1], w_odd, b_odd, o_q_norm[0][None, :], o_kv_norm[0][None, :], wuq_ext, wuk,
                               place, wuvt, cq, sq, ck, sk, l_lat, tk)
    att1 = _flash(q1, k1, vt1, l_lat, tk)
    conv1 = _conformer(glu, o_dw_w[0], o_dw_b[0][None, :], o_cln_g[0][None, :], o_cln_b[0][None, :], l_lat)
    w_r, b_r = _router_weights(moe_w_group[1], moe_b_group[1], moe_w_router[1], moe_b_router[1])
    x3, f, route = _post_mix(att1, conv1, x2, mods[1], o_w_out[0].astype(BF16), o_b_out[0][None, :],
                             ln_g[1, 0][None, :], ln_b[1, 0][None, :], w_r, b_r, l_lat, l_lat)
    y0, y1 = _moe(f, route, moe_w1[1].astype(BF16), moe_w3[1].astype(BF16), moe_w2[1].astype(BF16))
    return _post_ffn(x3, y0, y1, mods[1], ln_g[1, 1][None, :], ln_b[1, 1][None, :], l_lat)
```

```python
import functools

import numpy as np
import jax
import jax.numpy as jnp
from jax import lax
from jax.experimental import pallas as pl
from jax.experimental.pallas import tpu as pltpu

F32 = jnp.float32
BF16 = jnp.bfloat16

D_MODEL = 1024
DEPTH = 2
GRID_W = 64
ROPE_THETA = 10000.0
LN_EPS = 1e-6
NEG_INF = -1e30
DN_ALPHA = (2 * DEPTH) ** 0.25
LOG2E = 1.4426950408889634

LRU_WIDTH = 512
LRU_BLOCKS = 8
LRU_BLOCK_DIM = LRU_WIDTH // LRU_BLOCKS
LRU_CONV_W = 4
LRU_C = 8.0
SWA_HEADS = 8
SWA_KV_HEADS = 2
SWA_HEAD_DIM = 64
WINDOW = 128
MLA_HEADS = 8
MLA_Q_RANK = 256
MLA_KV_RANK = 128
MLA_NOPE = 64
MLA_ROPE = 32
MLA_V = 64
CONF_CH = 512
CONF_K = 31
N_GROUPS = 4
EXPERTS_PER_GROUP = 8
N_EXPERTS = N_GROUPS * EXPERTS_PER_GROUP
TOP_K = 2
D_EXPERT = 512

LANES = 128
SUBLANES = 8
ROW_TILE = 256
SCAN_SEG = ROW_TILE // SUBLANES
CONF_HALO = 16
CONF_SUB = 32
MOE_TILE = 256
FLASH_SUB = 256
FLASH_TQ = 512
VMEM_LIMIT = 48 * 1024 * 1024


def _cparams(n_axes, vmem=VMEM_LIMIT):
    return pltpu.CompilerParams(dimension_semantics=("arbitrary",) * n_axes, vmem_limit_bytes=vmem)


def _dot(a, b):
    return jnp.dot(a, b, preferred_element_type=F32)


def _dot_nt(a, b):
    return lax.dot_general(a, b, (((1,), (1,)), ((), ())), preferred_element_type=F32)


def _split_bf16(a):
    hi = a.astype(BF16)
    lo = (a - hi.astype(F32)).astype(BF16)
    return hi, lo


def _silu(x):
    return x * jax.nn.sigmoid(x)


def _gelu_tanh(x):
    return 0.5 * x * (1.0 + jnp.tanh(np.sqrt(2.0 / np.pi).astype(np.float32) * (x + 0.044715 * (x * x * x))))


def _layer_norm(x, g, b):
    mu = jnp.mean(x, axis=-1, keepdims=True)
    xc = x - mu
    var = jnp.mean(xc * xc, axis=-1, keepdims=True)
    return xc * lax.rsqrt(var + LN_EPS) * g + b


def _select_mod(mb_ref, mc_ref, idx, is_ctx):
    return jnp.where(is_ctx, mc_ref[0, idx:idx + 1, :], mb_ref[0, idx:idx + 1, :])


def _mod_kernel(c_ref, w_ref, b_ref, o_ref):
    c = c_ref[...]
    s_hi, s_lo = _split_bf16(_silu(c))
    w_hi, w_lo = _split_bf16(w_ref[0])
    o_ref[0] = _dot(s_hi, w_hi) + _dot(s_lo, w_hi) + _dot(s_hi, w_lo) + b_ref[0]


def _modulation(cond8, w_mod, b_mod):
    depth, d, n = w_mod.shape
    tn = 1536
    return pl.pallas_call(
        _mod_kernel,
        grid=(depth, n // tn),
        in_specs=[pl.BlockSpec((SUBLANES, d), lambda l, j: (0, 0)),
                  pl.BlockSpec((1, d, tn), lambda l, j: (l, 0, j)),
                  pl.BlockSpec((1, 1, tn), lambda l, j: (l, 0, j))],
        out_specs=pl.BlockSpec((1, SUBLANES, tn), lambda l, j: (l, 0, j)),
        out_shape=jax.ShapeDtypeStruct((depth, SUBLANES, n), F32),
        compiler_params=_cparams(2),
        name="modulation",
    )(cond8, w_mod, b_mod.reshape(depth, 1, n))


def _rope_pattern(width, row, col):
    d = width // 2
    half = d // 2
    inv = ROPE_THETA ** (-jnp.arange(half, dtype=F32) / half)
    ang_r = row.astype(F32)[:, None] * inv[None, :]
    ang_c = col.astype(F32)[:, None] * inv[None, :]
    cos = jnp.concatenate([jnp.cos(ang_r)] * 2 + [jnp.cos(ang_c)] * 2, axis=1)
    sin = jnp.concatenate([-jnp.sin(ang_r), jnp.sin(ang_r), -jnp.sin(ang_c), jnp.sin(ang_c)], axis=1)
    partner = np.concatenate([np.arange(half) + half, np.arange(half), d + np.arange(half) + half, d + np.arange(half)])
    return cos, sin, partner


def _in_even_kernel(x_ref, mb_ref, mc_ref, w_ref, b_ref, cos_ref, sin_ref,
                    g_ref, u_ref, q_ref, k_ref, v_ref, *, l_lat):
    i = pl.program_id(0)
    tm = x_ref.shape[1]
    rows = i * tm + lax.broadcasted_iota(jnp.int32, (tm, 1), 0)
    is_ctx = rows >= l_lat
    shift = _select_mod(mb_ref, mc_ref, 0, is_ctx)
    scale = _select_mod(mb_ref, mc_ref, 1, is_ctx)
    h = (x_ref[0] * (1.0 + scale) + shift).astype(BF16)

    def proj(lo, hi):
        return _dot(h, w_ref[:, lo:hi]) + b_ref[:, lo:hi]

    w = LRU_WIDTH
    g_ref[0] = proj(0, w)
    u_ref[0] = proj(w, 2 * w)
    cos = cos_ref[...]
    sin = sin_ref[...]
    cos4 = jnp.concatenate([cos] * 4, axis=1)
    sin4 = jnp.concatenate([sin] * 4, axis=1)
    o = 2 * w
    q = (proj(o, o + 512) * cos4 + proj(o + 512, o + 1024) * sin4) * (SWA_HEAD_DIM ** -0.5)
    q_ref[0] = q.astype(BF16)
    o += 1024
    cos2 = jnp.concatenate([cos] * 2, axis=1)
    sin2 = jnp.concatenate([sin] * 2, axis=1)
    k = proj(o, o + 256) * cos2 + proj(o + 256, o + 512) * sin2
    k_ref[0] = k.astype(BF16)
    o += 512
    v_ref[0] = proj(o, o + 256).astype(BF16)


def _in_even(x_all, mods, w, b, cos, sin, l_lat):
    bsz, s_all, d = x_all.shape
    tm = ROW_TILE
    nw = w.shape[1]
    row = lambda n: pl.BlockSpec((1, tm, n), lambda i, bb: (bb, i, 0))
    return pl.pallas_call(
        functools.partial(_in_even_kernel, l_lat=l_lat),
        grid=(s_all // tm, bsz),
        in_specs=[row(d),
                  pl.BlockSpec((1, 6, d), lambda i, bb: (bb, 0, 0)),
                  pl.BlockSpec((1, 6, d), lambda i, bb: (4, 0, 0)),
                  pl.BlockSpec((d, nw), lambda i, bb: (0, 0)),
                  pl.BlockSpec((1, nw), lambda i, bb: (0, 0)),
                  pl.BlockSpec((tm, LANES), lambda i, bb: (i, 0)),
                  pl.BlockSpec((tm, LANES), lambda i, bb: (i, 0))],
        out_specs=[row(512), row(512), row(512), row(256), row(256)],
        out_shape=[jax.ShapeDtypeStruct((bsz, s_all, 512), F32),
                   jax.ShapeDtypeStruct((bsz, s_all, 512), F32),
                   jax.ShapeDtypeStruct((bsz, s_all, 512), BF16),
                   jax.ShapeDtypeStruct((bsz, s_all, 256), BF16),
                   jax.ShapeDtypeStruct((bsz, s_all, 256), BF16)],
        compiler_params=_cparams(2),
        name="in_even",
    )(x_all, mods, mods, w, b, cos, sin)


def _softplus(x):
    return jnp.maximum(x, 0.0) + jnp.log1p(jnp.exp(-jnp.abs(x)))


def _lru_chunk_index(s, n_lat, reverse):
    if reverse:
        return jnp.where(s == 0, n_lat, n_lat - s)
    return jnp.where(s == 0, n_lat, s - 1)


def _lru_kernel(*refs, reverse, n_lat):
    if reverse:
        (u_ref, up_ref, un_ref, cw_ref, cb_ref, w_ref, bias_ref, lam_ref, g_ref, hf_ref,
         o_ref, ext, a_s, b_s, p_s, h_s, carry) = refs
    else:
        (u_ref, up_ref, un_ref, cw_ref, cb_ref, w_ref, bias_ref, lam_ref,
         o_ref, ext, a_s, b_s, p_s, h_s, carry) = refs
    s = pl.program_id(1)
    j = _lru_chunk_index(s, n_lat, reverse)
    t = ROW_TILE
    width = LRU_WIDTH

    @pl.when(s == 0)
    def _():
        carry[...] = jnp.zeros_like(carry)

    prev_ok = jnp.logical_and(j != 0, j != n_lat)
    next_ok = jnp.logical_and(j != n_lat - 1, j != n_lat)
    ext[0:SUBLANES, :] = jnp.where(prev_ok, up_ref[0], 0.0)
    ext[SUBLANES:SUBLANES + t, :] = u_ref[0]
    ext[SUBLANES + t:, :] = jnp.where(next_ok, un_ref[0], 0.0)
    uc = cb_ref[...]
    for kk in range(LRU_CONV_W):
        uc = uc + cw_ref[kk:kk + 1, :] * ext[SUBLANES - 1 + kk:SUBLANES - 1 + kk + t, :]

    z = _dot(uc.astype(BF16), w_ref[...]) + bias_ref[...]
    r = jax.nn.sigmoid(z[:, :width])
    ig = jax.nn.sigmoid(z[:, width:])
    log_a = (-LRU_C) * r * _softplus(-lam_ref[...])
    a = jnp.exp(log_a)
    b = jnp.sqrt(-jnp.tanh(log_a) * (a * a + 1.0)) * (ig * uc)

    steps = range(SCAN_SEG - 1, -1, -1) if reverse else range(SCAN_SEG)
    segs = range(SUBLANES - 1, -1, -1) if reverse else range(SUBLANES)
    for cb in range(width // LANES):
        cols = slice(cb * LANES, (cb + 1) * LANES)
        a_s[cb] = a[:, cols]
        b_s[cb] = b[:, cols]
        h = jnp.zeros((SUBLANES, LANES), F32)
        p = jnp.ones((SUBLANES, LANES), F32)
        for i in steps:
            sl = pl.ds(i, SUBLANES, stride=SCAN_SEG)
            ai = a_s[cb, sl, :]
            h = ai * h + b_s[cb, sl, :]
            p = p * ai
            h_s[cb, sl, :] = h
            p_s[cb, sl, :] = p
        c = carry[:, cols]
        seg_in = [None] * SUBLANES
        for sg in segs:
            seg_in[sg] = c
            c = p[sg:sg + 1, :] * c + h[sg:sg + 1, :]
        carry[:, cols] = c
        cin = jnp.concatenate(seg_in, axis=0)
        for i in range(SCAN_SEG):
            sl = pl.ds(i, SUBLANES, stride=SCAN_SEG)
            h_s[cb, sl, :] = h_s[cb, sl, :] + p_s[cb, sl, :] * cin
    hs = jnp.concatenate([h_s[cb] for cb in range(width // LANES)], axis=1)
    if reverse:
        o_ref[0] = (_gelu_tanh(g_ref[0]) * (hf_ref[0] + hs)).astype(o_ref.dtype)
    else:
        o_ref[0] = hs


def _lru(u, g, hf, conv_w, conv_b, w, bias, lam, reverse, l_lat):
    bsz, s_all, width = u.shape
    t = ROW_TILE
    n_chunks = s_all // t
    n_lat = l_lat // t
    hb = t // SUBLANES
    n_hb = s_all // SUBLANES
    cidx = functools.partial(_lru_chunk_index, n_lat=n_lat, reverse=reverse)
    chunk = pl.BlockSpec((1, t, width), lambda bb, s: (bb, cidx(s), 0))
    const = lambda shape: pl.BlockSpec(shape, lambda bb, s: (0,) * len(shape))
    in_specs = [chunk,
                pl.BlockSpec((1, SUBLANES, width), lambda bb, s: (bb, jnp.maximum(cidx(s) * hb - 1, 0), 0)),
                pl.BlockSpec((1, SUBLANES, width), lambda bb, s: (bb, jnp.minimum(cidx(s) * hb + hb, n_hb - 1), 0)),
                const((LRU_CONV_W, width)), const((1, width)), const((width, 2 * width)),
                const((1, 2 * width)), const((1, width))]
    args = [u, u, u, conv_w, conv_b, w, bias, lam]
    if reverse:
        in_specs += [chunk, chunk]
        args += [g, hf]
    return pl.pallas_call(
        functools.partial(_lru_kernel, reverse=reverse, n_lat=n_lat),
        grid=(bsz, n_chunks),
        in_specs=in_specs,
        out_specs=chunk,
        out_shape=jax.ShapeDtypeStruct((bsz, s_all, width), BF16 if reverse else F32),
        scratch_shapes=[pltpu.VMEM((t + 2 * SUBLANES, width), F32)]
                       + [pltpu.VMEM((width // LANES, t, LANES), F32)] * 4 + [pltpu.VMEM((1, width), F32)],
        compiler_params=_cparams(2),
        name="lru_bwd" if reverse else "lru_fwd",
    )(*args)


def _swa_kernel(sink_ref, q_ref, k_ref, v_ref, o_ref, *, l_lat, n_ctx):
    i = pl.program_id(1)
    tq = q_ref.shape[1]
    wk = tq + 2 * WINDOW
    is_lat = i < l_lat // tq
    ws = pl.multiple_of(jnp.clip(i * tq - WINDOW, 0, l_lat - wk), WINDOW)
    qpos = i * tq + lax.broadcasted_iota(jnp.int32, (tq, 1), 0)
    kpos = ws + lax.broadcasted_iota(jnp.int32, (1, wk), 1)
    band = jnp.logical_and(jnp.abs(qpos - kpos) <= WINDOW, is_lat)
    lo = lax.broadcasted_iota(jnp.int32, (1, LANES), 1) < SWA_HEAD_DIM
    heads_per_kv = SWA_HEADS // SWA_KV_HEADS
    for kh in range(SWA_KV_HEADS):
        cols = slice(LANES * kh, LANES * (kh + 1))
        kw = k_ref[0, pl.ds(ws, wk), cols]
        vw = v_ref[0, pl.ds(ws, wk), cols]
        kc = k_ref[0, l_lat:l_lat + n_ctx, cols]
        vc = v_ref[0, l_lat:l_lat + n_ctx, cols]
        for jq in range(heads_per_kv // 2):
            blk = kh * (heads_per_kv // 2) + jq
            qb = q_ref[0, :, LANES * blk:LANES * (blk + 1)]
            halves = []
            for half in range(2):
                qm = jnp.where(lo if half == 0 else jnp.logical_not(lo), qb, jnp.zeros_like(qb))
                sw = jnp.where(band, _dot_nt(qm, kw), NEG_INF)
                sc = _dot_nt(qm, kc)
                sink = sink_ref[2 * blk + half]
                m = jnp.maximum(jnp.maximum(jnp.max(sw, axis=-1, keepdims=True),
                                            jnp.max(sc, axis=-1, keepdims=True)), sink)
                pw = jnp.exp(sw - m)
                pc = jnp.exp(sc - m)
                den = (jnp.sum(pw, axis=-1, keepdims=True) + jnp.sum(pc, axis=-1, keepdims=True)
                       + jnp.exp(sink - m))
                halves.append((_dot(pw.astype(BF16), vw) + _dot(pc.astype(BF16), vc)) / den)
            o_ref[0, :, LANES * blk:LANES * (blk + 1)] = jnp.where(lo, halves[0], halves[1]).astype(o_ref.dtype)


def _swa(q, k2, v2, sink, l_lat):
    bsz, s_all, _ = q.shape
    tq = ROW_TILE
    n_ctx = s_all - l_lat
    return pl.pallas_call(
        functools.partial(_swa_kernel, l_lat=l_lat, n_ctx=n_ctx),
        grid_spec=pltpu.PrefetchScalarGridSpec(
            num_scalar_prefetch=1,
            grid=(bsz, s_all // tq),
            in_specs=[pl.BlockSpec((1, tq, 512), lambda bb, i, s: (bb, i, 0)),
                      pl.BlockSpec((1, s_all, 256), lambda bb, i, s: (bb, 0, 0)),
                      pl.BlockSpec((1, s_all, 256), lambda bb, i, s: (bb, 0, 0))],
            out_specs=pl.BlockSpec((1, tq, 512), lambda bb, i, s: (bb, i, 0))),
        out_shape=jax.ShapeDtypeStruct((bsz, s_all, 512), BF16),
        compiler_params=_cparams(2),
        name="swa",
    )(sink, q, k2, v2)


def _route(logits):
    lane = lax.broadcasted_iota(jnp.int32, logits.shape, 1)
    big = jnp.int32(2 ** 30)

    def first_argmax(vals, mask):
        v = jnp.where(mask, vals, NEG_INF)
        m = jnp.max(v, axis=-1, keepdims=True)
        idx = jnp.min(jnp.where(jnp.logical_and(mask, v == m), lane, big), axis=-1, keepdims=True)
        return m, idx

    gmask = lane < N_GROUPS
    gm, gidx = first_argmax(logits, gmask)
    g_prob = 1.0 / jnp.sum(jnp.where(gmask, jnp.exp(logits - gm), 0.0), axis=-1, keepdims=True)
    base = N_GROUPS + gidx * EXPERTS_PER_GROUP
    emask = jnp.logical_and(lane >= base, lane < base + EXPERTS_PER_GROUP)
    m0, i0 = first_argmax(logits, emask)
    m1, i1 = first_argmax(logits, jnp.logical_and(emask, lane != i0))
    w1 = jnp.exp(m1 - m0)
    inv = g_prob / (1.0 + w1)
    return i0 - N_GROUPS, i1 - N_GROUPS, inv, w1 * inv


def _post_mix_kernel(a_ref, b_ref, x_ref, mb_ref, mc_ref, w_ref, bo_ref, lng_ref, lnb_ref, wr_ref, br_ref,
                     x1_ref, f_ref, r_ref, *, l_lat):
    i = pl.program_id(1)
    tm = x_ref.shape[1]
    rows = i * tm + lax.broadcasted_iota(jnp.int32, (tm, 1), 0)
    is_ctx = rows >= l_lat
    half = a_ref.shape[2]
    y = _dot(a_ref[0], w_ref[0:half, :]) + _dot(b_ref[0], w_ref[half:, :]) + bo_ref[...]
    gate = _select_mod(mb_ref, mc_ref, 2, is_ctx)
    x1 = _layer_norm(DN_ALPHA * x_ref[0] + gate * y, lng_ref[...], lnb_ref[...])
    x1_ref[0] = x1
    f = x1 * (1.0 + _select_mod(mb_ref, mc_ref, 4, is_ctx)) + _select_mod(mb_ref, mc_ref, 3, is_ctx)
    f_ref[0] = f.astype(BF16)
    f_hi, f_lo = _split_bf16(f)
    two = _dot(f_hi, wr_ref[...])
    logits = two[:, :LANES] + two[:, LANES:] + _dot(f_lo, wr_ref[:, :LANES]) + br_ref[...]
    e0, e1, g0, g1 = _route(logits)
    lane = lax.broadcasted_iota(jnp.int32, (tm, LANES), 1)
    r_ref[0] = jnp.where(lane == 0, e0.astype(F32),
                         jnp.where(lane == 1, e1.astype(F32), jnp.where(lane == 2, g0, jnp.where(lane == 3, g1, 0.0))))


def _post_mix(a, b, x, mods, w_out, b_out, ln_g, ln_b, w_r, b_r, n_rows, l_lat):
    bsz = x.shape[0]
    d = x.shape[2]
    tm = ROW_TILE
    half = a.shape[2]
    row = lambda n: pl.BlockSpec((1, tm, n), lambda bb, i: (bb, i, 0))
    const = lambda shape: pl.BlockSpec(shape, lambda bb, i: (0,) * len(shape))
    return pl.pallas_call(
        functools.partial(_post_mix_kernel, l_lat=l_lat),
        grid=(bsz, n_rows // tm),
        in_specs=[row(half), row(half), row(d),
                  pl.BlockSpec((1, 6, d), lambda bb, i: (bb, 0, 0)),
                  pl.BlockSpec((1, 6, d), lambda bb, i: (4, 0, 0)),
                  const((2 * half, d)), const((1, d)), const((1, d)), const((1, d)),
                  const((d, 2 * LANES)), const((1, LANES))],
        out_specs=[row(d), row(d), row(LANES)],
        out_shape=[jax.ShapeDtypeStruct((bsz, n_rows, d), F32),
                   jax.ShapeDtypeStruct((bsz, n_rows, d), BF16),
                   jax.ShapeDtypeStruct((bsz, n_rows, LANES), F32)],
        compiler_params=_cparams(2),
        name="post_mix",
    )(a, b, x, mods, mods, w_out, b_out, ln_g, ln_b, w_r, b_r)


def _expert_kernel(be_ref, nb_ref, x_ref, gate_ref, w1_ref, w3_ref, w2_ref, o_ref):
    i = pl.program_id(0)

    @pl.when(i < nb_ref[0])
    def _():
        xb = x_ref[...]
        h1 = _dot(xb, w1_ref[0])
        h3 = _dot(xb, w3_ref[0])
        act = (_silu(h1) * h3).astype(BF16)
        o_ref[...] = _dot(act, w2_ref[0]) * gate_ref[...]

    @pl.when(i >= nb_ref[0])
    def _():
        o_ref[...] = jnp.zeros_like(o_ref)


def _experts(block_expert, n_used, xb, slot_gate, w1, w3, w2):
    n_slots, d = xb.shape
    tb = MOE_TILE
    de = w1.shape[2]
    return pl.pallas_call(
        _expert_kernel,
        grid_spec=pltpu.PrefetchScalarGridSpec(
            num_scalar_prefetch=2,
            grid=(n_slots // tb,),
            in_specs=[pl.BlockSpec((tb, d), lambda i, be, nb: (i, 0)),
                      pl.BlockSpec((tb, 1), lambda i, be, nb: (i, 0)),
                      pl.BlockSpec((1, d, de), lambda i, be, nb: (be[i], 0, 0)),
                      pl.BlockSpec((1, d, de), lambda i, be, nb: (be[i], 0, 0)),
                      pl.BlockSpec((1, de, d), lambda i, be, nb: (be[i], 0, 0))],
            out_specs=pl.BlockSpec((tb, d), lambda i, be, nb: (i, 0))),
        out_shape=jax.ShapeDtypeStruct((n_slots, d), F32),
        compiler_params=_cparams(1),
        name="experts",
    )(block_expert, n_used, xb, slot_gate, w1, w3, w2)


def _post_ffn_kernel(x_ref, y0_ref, y1_ref, mb_ref, mc_ref, lng_ref, lnb_ref, o_ref, *, l_lat):
    i = pl.program_id(1)
    tm = x_ref.shape[1]
    rows = i * tm + lax.broadcasted_iota(jnp.int32, (tm, 1), 0)
    is_ctx = rows >= l_lat
    gate = _select_mod(mb_ref, mc_ref, 5, is_ctx)
    ffn = y0_ref[0] + y1_ref[0]
    o_ref[0] = _layer_norm(DN_ALPHA * x_ref[0] + gate * ffn, lng_ref[...], lnb_ref[...])


def _post_ffn(x1, y0, y1, mods, ln_g, ln_b, l_lat):
    bsz, n_rows, d = x1.shape
    tm = ROW_TILE
    row = pl.BlockSpec((1, tm, d), lambda bb, i: (bb, i, 0))
    const = lambda shape: pl.BlockSpec(shape, lambda bb, i: (0,) * len(shape))
    return pl.pallas_call(
        functools.partial(_post_ffn_kernel, l_lat=l_lat),
        grid=(bsz, n_rows // tm),
        in_specs=[row, row, row,
                  pl.BlockSpec((1, 6, d), lambda bb, i: (bb, 0, 0)),
                  pl.BlockSpec((1, 6, d), lambda bb, i: (4, 0, 0)),
                  const((1, d)), const((1, d))],
        out_specs=row,
        out_shape=jax.ShapeDtypeStruct((bsz, n_rows, d), F32),
        compiler_params=_cparams(2),
        name="post_ffn",
    )(x1, y0, y1, mods, mods, ln_g, ln_b)


def _moe(f, route, w1, w3, w2):
    bsz, n_rows, d = f.shape
    n_tok = bsz * n_rows
    tb = MOE_TILE
    tokens = f.reshape(n_tok, d)
    r = route.reshape(n_tok, LANES)
    expert = r[:, 0:TOP_K].astype(jnp.int32).reshape(-1)
    gate = r[:, TOP_K:2 * TOP_K].reshape(-1)
    n_assign = n_tok * TOP_K
    token = jnp.repeat(jnp.arange(n_tok, dtype=jnp.int32), TOP_K)
    onehot = (expert[:, None] == jnp.arange(N_EXPERTS, dtype=jnp.int32)[None, :]).astype(jnp.int32)
    csum = jnp.cumsum(onehot, axis=0)
    rank = jnp.take_along_axis(csum, expert[:, None], axis=1)[:, 0] - 1
    counts = csum[-1]
    padded = (counts + tb - 1) // tb * tb
    p_ends = jnp.cumsum(padded)
    p_starts = p_ends - padded
    dest = p_starts[expert] + rank
    n_blocks = -(-(n_assign + N_EXPERTS * (tb - 1)) // tb)
    n_slots = n_blocks * tb
    slot_tok = jnp.zeros((n_slots,), jnp.int32).at[dest].set(token)
    slot_gate = jnp.zeros((n_slots,), F32).at[dest].set(gate)
    block_start = jnp.arange(n_blocks, dtype=jnp.int32) * tb
    block_expert = jnp.minimum(jnp.sum((p_ends[None, :] <= block_start[:, None]).astype(jnp.int32), axis=1),
                               N_EXPERTS - 1)
    n_used = (p_ends[-1] // tb).astype(jnp.int32).reshape(1)
    xb = tokens[slot_tok]
    yb = _experts(block_expert, n_used, xb, slot_gate[:, None], w1, w3, w2)
    y = yb[dest].reshape(n_tok, TOP_K, d)
    return y[:, 0].reshape(bsz, n_rows, d), y[:, 1].reshape(bsz, n_rows, d)


def _rms(x, g):
    return x * lax.rsqrt(jnp.mean(x * x, axis=-1, keepdims=True) + LN_EPS) * g

def _in_odd_kernel(x_ref, mb_ref, mc_ref, w_ref, b_ref, qn_ref, kvn_ref, wuq_ref, wuk_ref, place_ref, wuvt_ref,
                   cq_ref, sq_ref, ck_ref, sk_ref, q_ref, k_ref, vt_ref, glu_ref, *, l_lat):
    i = pl.program_id(0)
    tm = x_ref.shape[1]
    rows = i * tm + lax.broadcasted_iota(jnp.int32, (tm, 1), 0)
    is_ctx = rows >= l_lat
    shift = _select_mod(mb_ref, mc_ref, 0, is_ctx)
    scale = _select_mod(mb_ref, mc_ref, 1, is_ctx)
    h = (x_ref[0] * (1.0 + scale) + shift).astype(BF16)

    def proj(lo, hi):
        return _dot(h, w_ref[:, lo:hi]) + b_ref[:, lo:hi]

    cqn = _rms(proj(0, MLA_Q_RANK), qn_ref[...]).astype(BF16)
    hw = MLA_HEADS * LANES
    cos8 = jnp.concatenate([cq_ref[...]] * MLA_HEADS, axis=1)
    sin8 = jnp.concatenate([sq_ref[...]] * MLA_HEADS, axis=1)
    q = _dot(cqn, wuq_ref[:, :hw]) * cos8 + _dot(cqn, wuq_ref[:, hw:]) * sin8
    q_ref[0] = (q * ((MLA_NOPE + MLA_ROPE) ** -0.5 * LOG2E)).astype(BF16)
    o = MLA_Q_RANK
    cn = _rms(proj(o, o + MLA_KV_RANK), kvn_ref[...]).astype(BF16)
    o += MLA_KV_RANK
    kpe = proj(o, o + LANES) * ck_ref[...] + proj(o + LANES, o + 2 * LANES) * sk_ref[...]
    k_ref[0] = (_dot(cn, wuk_ref[...]) + _dot(kpe.astype(BF16), place_ref[...])).astype(BF16)
    vt = _dot_nt(wuvt_ref[...], cn)
    vt_ref[0] = vt.reshape(MLA_HEADS, 1, MLA_V, tm).astype(BF16)
    o += 2 * LANES
    glu_ref[0] = proj(o, o + CONF_CH) * jax.nn.sigmoid(proj(o + CONF_CH, o + 2 * CONF_CH))


def _in_odd(x_all, mods, w, b, qn, kvn, wuq, wuk, place, wuvt, cq, sq, ck, sk, l_lat, tm):
    bsz, s_all, d = x_all.shape
    nw = w.shape[1]
    nk = s_all // tm
    hw = MLA_HEADS * LANES
    row = lambda n: pl.BlockSpec((1, tm, n), lambda i, bb: (bb, i, 0))
    const = lambda shape: pl.BlockSpec(shape, lambda i, bb: (0,) * len(shape))
    tab = pl.BlockSpec((tm, LANES), lambda i, bb: (i, 0))
    return pl.pallas_call(
        functools.partial(_in_odd_kernel, l_lat=l_lat),
        grid=(nk, bsz),
        in_specs=[row(d),
                  pl.BlockSpec((1, 6, d), lambda i, bb: (bb, 0, 0)),
                  pl.BlockSpec((1, 6, d), lambda i, bb: (4, 0, 0)),
                  const((d, nw)), const((1, nw)), const((1, MLA_Q_RANK)), const((1, MLA_KV_RANK)),
                  const((MLA_Q_RANK, 2 * hw)), const((MLA_KV_RANK, hw)), const((LANES, hw)),
                  const((MLA_HEADS * MLA_V, MLA_KV_RANK)), tab, tab, tab, tab],
        out_specs=[row(hw), row(hw),
                   pl.BlockSpec((1, MLA_HEADS, 1, MLA_V, tm), lambda i, bb: (bb, 0, i, 0, 0)),
                   row(CONF_CH)],
        out_shape=[jax.ShapeDtypeStruct((bsz, s_all, hw), BF16),
                   jax.ShapeDtypeStruct((bsz, s_all, hw), BF16),
                   jax.ShapeDtypeStruct((bsz, MLA_HEADS, nk, MLA_V, tm), BF16),
                   jax.ShapeDtypeStruct((bsz, s_all, CONF_CH), F32)],
        compiler_params=_cparams(2),
        name="in_odd",
    )(x_all, mods, mods, w, b, qn, kvn, wuq, wuk, place, wuvt, cq, sq, ck, sk)


def _flash_kernel(q_ref, k_ref, vt_ref, o_ref, s_scr, *, tk, nk):
    tq = q_ref.shape[1]
    chains = [(hh, j) for hh in range(2) for j in range(tq // FLASH_SUB)]

    def scores(c, slot):
        off = c * tk if isinstance(c, int) else pl.multiple_of(c * tk, tk)
        for ci, (hh, j) in enumerate(chains):
            q = q_ref[0, j * FLASH_SUB:(j + 1) * FLASH_SUB, LANES * hh:LANES * (hh + 1)]
            k = k_ref[0, pl.ds(off, tk), LANES * hh:LANES * (hh + 1)]
            s_scr[slot, ci] = _dot_nt(k, q)

    def update(c, slot, carry):
        probs = []
        for ci, (m, l, acc) in enumerate(carry):
            m_new = jnp.maximum(m, jnp.max(s_scr[slot, ci], axis=0, keepdims=True))
            alpha = jnp.exp2(m - m_new)
            p = jnp.exp2(s_scr[slot, ci] - m_new)
            probs.append((m_new, alpha, alpha * l + jnp.sum(p, axis=0, keepdims=True), p.astype(BF16)))
        out = []
        for (hh, j), (m_new, alpha, l, p), (_, _, acc) in zip(chains, probs, carry):
            out.append((m_new, l, alpha * acc + _dot(vt_ref[0, hh, c], p)))
        return tuple(out)

    def pair(i, carry):
        c = 2 * i
        scores(c + 1, 1)
        carry = update(c, 0, carry)
        scores(c + 2, 0)
        return update(c + 1, 1, carry)

    init = tuple((jnp.full((1, FLASH_SUB), NEG_INF, F32), jnp.zeros((1, FLASH_SUB), F32),
                  jnp.zeros((MLA_V, FLASH_SUB), F32)) for _ in chains)
    scores(0, 0)
    n_pairs = (nk - 1) // 2
    res = lax.fori_loop(0, n_pairs, pair, init)
    if nk - 2 * n_pairs == 2:
        scores(nk - 1, 1)
        res = update(nk - 2, 0, res)
        res = update(nk - 1, 1, res)
    else:
        res = update(nk - 1, 0, res)
    norm = {ch: acc / l for ch, (m, l, acc) in zip(chains, res)}
    for j in range(tq // FLASH_SUB):
        pair = jnp.concatenate([norm[(0, j)], norm[(1, j)]], axis=0)
        o_ref[0, j * FLASH_SUB:(j + 1) * FLASH_SUB, :] = pair.T.astype(o_ref.dtype)


def _flash(q, k, vt, l_lat, tk):
    bsz, s_all, _ = q.shape
    nk = s_all // tk
    tq = FLASH_TQ if l_lat % FLASH_TQ == 0 else FLASH_SUB
    return pl.pallas_call(
        functools.partial(_flash_kernel, tk=tk, nk=nk),
        grid=(bsz, MLA_HEADS // 2, l_lat // tq),
        in_specs=[pl.BlockSpec((1, tq, 2 * LANES), lambda bb, hp, i: (bb, i, hp)),
                  pl.BlockSpec((1, s_all, 2 * LANES), lambda bb, hp, i: (bb, 0, hp)),
                  pl.BlockSpec((1, 2, nk, MLA_V, tk), lambda bb, hp, i: (bb, hp, 0, 0, 0))],
        out_specs=pl.BlockSpec((1, tq, LANES), lambda bb, hp, i: (bb, i, hp)),
        out_shape=jax.ShapeDtypeStruct((bsz, l_lat, MLA_HEADS * MLA_V), BF16),
        scratch_shapes=[pltpu.VMEM((2, 2 * tq // FLASH_SUB, tk, FLASH_SUB), F32)],
        compiler_params=_cparams(3),
        name="mla_attention",
    )(q, k, vt)


def _conf_kernel(x_ref, xp_ref, xn_ref, w_ref, b_ref, g_ref, bb_ref, o_ref, ext):
    i = pl.program_id(1)
    n = pl.num_programs(1)
    t = x_ref.shape[1]
    hl = CONF_HALO
    ext[0:hl, :] = jnp.where(i > 0, xp_ref[0], 0.0)
    ext[hl:hl + t, :] = x_ref[0]
    ext[hl + t:, :] = jnp.where(i < n - 1, xn_ref[0], 0.0)
    off = hl - CONF_K // 2
    for r in range(t // CONF_SUB):
        acc = jnp.zeros((CONF_SUB, CONF_CH), F32) + b_ref[...]
        for kk in range(CONF_K):
            st = off + kk + r * CONF_SUB
            acc = acc + w_ref[kk:kk + 1, :] * ext[st:st + CONF_SUB, :]
        o_ref[0, r * CONF_SUB:(r + 1) * CONF_SUB, :] = _silu(_layer_norm(acc, g_ref[...], bb_ref[...])).astype(o_ref.dtype)


def _conformer(glu, w, b, g, beta, l_lat):
    bsz, _, ch = glu.shape
    t = ROW_TILE
    hl = CONF_HALO
    hb = t // hl
    n_hb = l_lat // hl
    const = lambda shape: pl.BlockSpec(shape, lambda bb, i: (0,) * len(shape))
    return pl.pallas_call(
        _conf_kernel,
        grid=(bsz, l_lat // t),
        in_specs=[pl.BlockSpec((1, t, ch), lambda bb, i: (bb, i, 0)),
                  pl.BlockSpec((1, hl, ch), lambda bb, i: (bb, jnp.maximum(i * hb - 1, 0), 0)),
                  pl.BlockSpec((1, hl, ch), lambda bb, i: (bb, jnp.minimum(i * hb + hb, n_hb - 1), 0)),
                  const((CONF_K, ch)), const((1, ch)), const((1, ch)), const((1, ch))],
        out_specs=pl.BlockSpec((1, t, ch), lambda bb, i: (bb, i, 0)),
        out_shape=jax.ShapeDtypeStruct((bsz, l_lat, ch), BF16),
        scratch_shapes=[pltpu.VMEM((t + 2 * hl, ch), F32)],
        compiler_params=_cparams(2),
        name="conformer_conv",
    )(glu, glu, glu, w, b, g, beta)


def _block_diag(w):
    nb, c, d = w.shape
    out = jnp.zeros((nb, c, nb, d), w.dtype)
    out = out.at[jnp.arange(nb), :, jnp.arange(nb), :].set(w)
    return out.reshape(nb * c, nb * d)


def _router_weights(w_group, b_group, w_router, b_router):
    d = w_group.shape[0]
    w = jnp.zeros((d, LANES), F32).at[:, :N_GROUPS].set(w_group).at[:, N_GROUPS:N_GROUPS + N_EXPERTS].set(w_router)
    hi, lo = _split_bf16(w)
    b = jnp.zeros((1, LANES), F32).at[0, :N_GROUPS].set(b_group).at[0, N_GROUPS:N_GROUPS + N_EXPERTS].set(b_router)
    return jnp.concatenate([hi, lo], axis=1), b


def kernel(x, c, ctx, c_ctx, w_mod, b_mod, ln_g, ln_b, e_w_in, e_b_in, e_conv_w, e_conv_b, e_lru_wa, e_lru_ba, e_lru_wx, e_lru_bx, e_lru_lambda, e_sink, e_w_out, e_b_out, o_w_in, o_b_in, o_q_norm, o_kv_norm, o_w_uq, o_w_uk, o_w_uv, o_dw_w, o_dw_b, o_cln_g, o_cln_b, o_w_out, o_b_out, moe_w_group, moe_b_group, moe_w_router, moe_b_router, moe_w1, moe_w3, moe_w2):
    bsz, l_lat, d = x.shape
    n_ctx = ctx.shape[1]
    s_all = l_lat + n_ctx
    assert d == D_MODEL and bsz <= 4 and l_lat % ROW_TILE == 0 and n_ctx == ROW_TILE and l_lat % GRID_W == 0
    assert w_mod.shape[0] == DEPTH == 2

    x_all = jnp.concatenate([x, ctx], axis=1)
    cond8 = jnp.zeros((SUBLANES, d), F32).at[:bsz].set(c).at[4].set(c_ctx)
    mods = _modulation(cond8, w_mod, b_mod).reshape(DEPTH, SUBLANES, 6, d)

    t_idx = jnp.arange(l_lat, dtype=jnp.int32)
    row_pos, col_pos = t_idx // GRID_W, t_idx % GRID_W

    cos64, sin64, partner64 = _rope_pattern(SWA_HEAD_DIM, row_pos, col_pos)
    cos_e = jnp.ones((s_all, LANES), F32).at[:l_lat].set(jnp.concatenate([cos64] * 2, axis=1))
    sin_e = jnp.zeros((s_all, LANES), F32).at[:l_lat].set(jnp.concatenate([sin64] * 2, axis=1))
    w_in, b_in = e_w_in[0], e_b_in[0]
    oq = 2 * LRU_WIDTH
    ok = oq + SWA_HEADS * SWA_HEAD_DIM
    ov = ok + SWA_KV_HEADS * SWA_HEAD_DIM
    q_cols = oq + np.arange(SWA_HEADS * SWA_HEAD_DIM)
    qp_cols = oq + (np.arange(SWA_HEADS)[:, None] * SWA_HEAD_DIM + partner64[None, :]).reshape(-1)
    dup = np.repeat(np.arange(SWA_KV_HEADS), 2)
    k_cols = ok + (dup[:, None] * SWA_HEAD_DIM + np.arange(SWA_HEAD_DIM)[None, :]).reshape(-1)
    kp_cols = ok + (dup[:, None] * SWA_HEAD_DIM + partner64[None, :]).reshape(-1)
    v_cols = ov + (dup[:, None] * SWA_HEAD_DIM + np.arange(SWA_HEAD_DIM)[None, :]).reshape(-1)
    cols = np.concatenate([np.arange(oq), q_cols, qp_cols, k_cols, kp_cols, v_cols])
    g0, u0, q0, k0, v0 = _in_even(x_all, mods[0], w_in[:, cols].astype(BF16), b_in[cols][None, :], cos_e, sin_e, l_lat)

    def lru_weights(dd):
        w = jnp.concatenate([_block_diag(e_lru_wa[0, dd]), _block_diag(e_lru_wx[0, dd])], axis=1).astype(BF16)
        bias = jnp.concatenate([e_lru_ba[0, dd], e_lru_bx[0, dd]])[None, :]
        return w, bias, e_lru_lambda[0, dd][None, :]

    cw, cb = e_conv_w[0], e_conv_b[0][None, :]
    hf = _lru(u0, None, None, cw, cb, *lru_weights(0), reverse=False, l_lat=l_lat)
    rec = _lru(u0, g0, hf, cw, cb, *lru_weights(1), reverse=True, l_lat=l_lat)
    att = _swa(q0, k0, v0, e_sink[0], l_lat)
    w_r, b_r = _router_weights(moe_w_group[0], moe_b_group[0], moe_w_router[0], moe_b_router[0])
    x1, f, route = _post_mix(rec, att, x_all, mods[0], e_w_out[0].astype(BF16), e_b_out[0][None, :],
                             ln_g[0, 0][None, :], ln_b[0, 0][None, :], w_r, b_r, s_all, l_lat)
    y0, y1 = _moe(f, route, moe_w1[0].astype(BF16), moe_w3[0].astype(BF16), moe_w2[0].astype(BF16))
    x2 = _post_ffn(x1, y0, y1, mods[0], ln_g[0, 1][None, :], ln_b[0, 1][None, :], l_lat)

    tk = 768 if s_all % 768 == 0 else ROW_TILE
    cos32, sin32, partner32 = _rope_pattern(MLA_ROPE, row_pos, col_pos)
    w_in, b_in = o_w_in[0], o_b_in[0]
    o_pe = MLA_Q_RANK + MLA_KV_RANK
    o_cv = o_pe + MLA_ROPE

    def pad_cols(wm, bv, n):
        return (jnp.zeros((wm.shape[0], n), F32).at[:, :wm.shape[1]].set(wm),
                jnp.zeros((n,), F32).at[:bv.shape[0]].set(bv))

    pe_cols = o_pe + np.arange(MLA_ROPE)
    wa_, ba_ = pad_cols(w_in[:, pe_cols], b_in[pe_cols], LANES)
    wb_, bb_ = pad_cols(w_in[:, o_pe + partner32], b_in[o_pe + partner32], LANES)
    w_odd = jnp.concatenate([w_in[:, :o_pe], wa_, wb_, w_in[:, o_cv:]], axis=1).astype(BF16)
    b_odd = jnp.concatenate([b_in[:o_pe], ba_, bb_, b_in[o_cv:]])[None, :]
    ck = jnp.ones((s_all, LANES), F32).at[:l_lat, :MLA_ROPE].set(cos32)
    sk = jnp.zeros((s_all, LANES), F32).at[:l_lat, :MLA_ROPE].set(sin32)
    cq = jnp.ones((s_all, LANES), F32).at[:l_lat, MLA_NOPE:MLA_NOPE + MLA_ROPE].set(cos32)
    sq = jnp.zeros((s_all, LANES), F32).at[:l_lat, MLA_NOPE:MLA_NOPE + MLA_ROPE].set(sin32)
    hd = MLA_NOPE + MLA_ROPE
    wuq = o_w_uq[0].reshape(MLA_Q_RANK, MLA_HEADS, hd)
    wuq_main = jnp.zeros((MLA_Q_RANK, MLA_HEADS, LANES), F32).at[:, :, :hd].set(wuq)
    wuq_part = jnp.zeros((MLA_Q_RANK, MLA_HEADS, LANES), F32).at[:, :, MLA_NOPE:hd].set(wuq[:, :, MLA_NOPE + partner32])
    hw = MLA_HEADS * LANES
    wuq_ext = jnp.concatenate([wuq_main.reshape(MLA_Q_RANK, hw), wuq_part.reshape(MLA_Q_RANK, hw)], axis=1).astype(BF16)
    wuk = jnp.zeros((MLA_KV_RANK, MLA_HEADS, LANES), F32).at[:, :, :MLA_NOPE].set(
        o_w_uk[0].reshape(MLA_KV_RANK, MLA_HEADS, MLA_NOPE)).reshape(MLA_KV_RANK, hw).astype(BF16)
    place = np.zeros((LANES, MLA_HEADS, LANES), np.float32)
    for hh in range(MLA_HEADS):
        place[np.arange(MLA_ROPE), hh, MLA_NOPE + np.arange(MLA_ROPE)] = 1.0
    place = jnp.asarray(place.reshape(LANES, hw), BF16)
    wuvt = o_w_uv[0].T.astype(BF16)
    q1, k1, vt1, glu = _in_odd(x2, mods[1], w_odd, b_odd, o_q_norm[0][None, :], o_kv_norm[0][None, :], wuq_ext, wuk,
                               place, wuvt, cq, sq, ck, sk, l_lat, tk)
    att1 = _flash(q1, k1, vt1, l_lat, tk)
    conv1 = _conformer(glu, o_dw_w[0], o_dw_b[0][None, :], o_cln_g[0][None, :], o_cln_b[0][None, :], l_lat)
    w_r, b_r = _router_weights(moe_w_group[1], moe_b_group[1], moe_w_router[1], moe_b_router[1])
    x3, f, route = _post_mix(att1, conv1, x2, mods[1], o_w_out[0].astype(BF16), o_b_out[0][None, :],
                             ln_g[1, 0][None, :], ln_b[1, 0][None, :], w_r, b_r, l_lat, l_lat)
    y0, y1 = _moe(f, route, moe_w1[1].astype(BF16), moe_w3[1].astype(BF16), moe_w2[1].astype(BF16))
    return _post_ffn(x3, y0, y1, mods[1], ln_g[1, 1][None, :], ln_b[1, 1][None, :], l_lat)
```

```python
import functools

import numpy as np
import jax
import jax.numpy as jnp
from jax import lax
from jax.experimental import pallas as pl
from jax.experimental.pallas import tpu as pltpu

F32 = jnp.float32
BF16 = jnp.bfloat16

D_MODEL = 1024
DEPTH = 2
GRID_W = 64
ROPE_THETA = 10000.0
LN_EPS = 1e-6
NEG_INF = -1e30
DN_ALPHA = (2 * DEPTH) ** 0.25
LOG2E = 1.4426950408889634

LRU_WIDTH = 512
LRU_BLOCKS = 8
LRU_BLOCK_DIM = LRU_WIDTH // LRU_BLOCKS
LRU_CONV_W = 4
LRU_C = 8.0
SWA_HEADS = 8
SWA_KV_HEADS = 2
SWA_HEAD_DIM = 64
WINDOW = 128
MLA_HEADS = 8
MLA_Q_RANK = 256
MLA_KV_RANK = 128
MLA_NOPE = 64
MLA_ROPE = 32
MLA_V = 64
CONF_CH = 512
CONF_K = 31
N_GROUPS = 4
EXPERTS_PER_GROUP = 8
N_EXPERTS = N_GROUPS * EXPERTS_PER_GROUP
TOP_K = 2
D_EXPERT = 512

LANES = 128
SUBLANES = 8
ROW_TILE = 256
SCAN_SEG = ROW_TILE // SUBLANES
SCAN_PITCH = SCAN_SEG + 4
CONF_HALO = 16
CONF_SUB = 32
MOE_TILE = 256
FLASH_SUB = 256
FLASH_TQ = 1024
VMEM_LIMIT = 48 * 1024 * 1024


def _cparams(n_axes, vmem=VMEM_LIMIT):
    return pltpu.CompilerParams(dimension_semantics=("arbitrary",) * n_axes, vmem_limit_bytes=vmem)


def _dot(a, b):
    return jnp.dot(a, b, preferred_element_type=F32)


def _dot_nt(a, b):
    return lax.dot_general(a, b, (((1,), (1,)), ((), ())), preferred_element_type=F32)


def _split_bf16(a):
    hi = a.astype(BF16)
    lo = (a - hi.astype(F32)).astype(BF16)
    return hi, lo


def _silu(x):
    return x * jax.nn.sigmoid(x)


def _gelu_tanh(x):
    return 0.5 * x * (1.0 + jnp.tanh(np.sqrt(2.0 / np.pi).astype(np.float32) * (x + 0.044715 * (x * x * x))))


def _layer_norm(x, g, b):
    mu = jnp.mean(x, axis=-1, keepdims=True)
    xc = x - mu
    var = jnp.mean(xc * xc, axis=-1, keepdims=True)
    return xc * lax.rsqrt(var + LN_EPS) * g + b


def _select_mod(mb_ref, mc_ref, idx, is_ctx):
    return jnp.where(is_ctx, mc_ref[0, idx:idx + 1, :], mb_ref[0, idx:idx + 1, :])


def _mod_kernel(c_ref, w_ref, b_ref, o_ref):
    c = c_ref[...]
    s_hi, s_lo = _split_bf16(_silu(c))
    w_hi, w_lo = _split_bf16(w_ref[0])
    o_ref[0] = _dot(s_hi, w_hi) + _dot(s_lo, w_hi) + _dot(s_hi, w_lo) + b_ref[0]


def _modulation(cond8, w_mod, b_mod):
    depth, d, n = w_mod.shape
    tn = 1536
    return pl.pallas_call(
        _mod_kernel,
        grid=(depth, n // tn),
        in_specs=[pl.BlockSpec((SUBLANES, d), lambda l, j: (0, 0)),
                  pl.BlockSpec((1, d, tn), lambda l, j: (l, 0, j)),
                  pl.BlockSpec((1, 1, tn), lambda l, j: (l, 0, j))],
        out_specs=pl.BlockSpec((1, SUBLANES, tn), lambda l, j: (l, 0, j)),
        out_shape=jax.ShapeDtypeStruct((depth, SUBLANES, n), F32),
        compiler_params=_cparams(2),
        name="modulation",
    )(cond8, w_mod, b_mod.reshape(depth, 1, n))


def _rope_pattern(width, row, col):
    d = width // 2
    half = d // 2
    inv = ROPE_THETA ** (-jnp.arange(half, dtype=F32) / half)
    ang_r = row.astype(F32)[:, None] * inv[None, :]
    ang_c = col.astype(F32)[:, None] * inv[None, :]
    cos = jnp.concatenate([jnp.cos(ang_r)] * 2 + [jnp.cos(ang_c)] * 2, axis=1)
    sin = jnp.concatenate([-jnp.sin(ang_r), jnp.sin(ang_r), -jnp.sin(ang_c), jnp.sin(ang_c)], axis=1)
    partner = np.concatenate([np.arange(half) + half, np.arange(half), d + np.arange(half) + half, d + np.arange(half)])
    return cos, sin, partner


def _in_even_kernel(x_ref, c_ref, mb_ref, mc_ref, w_ref, b_ref, cos_ref, sin_ref,
                    g_ref, u_ref, q_ref, k_ref, v_ref, *, l_lat):
    i = pl.program_id(1)
    tm = x_ref.shape[1]
    rows = i * tm + lax.broadcasted_iota(jnp.int32, (tm, 1), 0)
    is_ctx = rows >= l_lat
    shift = _select_mod(mb_ref, mc_ref, 0, is_ctx)
    scale = _select_mod(mb_ref, mc_ref, 1, is_ctx)
    x_in = jnp.where(i * tm >= l_lat, c_ref[0], x_ref[0])
    h = (x_in * (1.0 + scale) + shift).astype(BF16)

    def proj(lo, hi):
        return _dot(h, w_ref[:, lo:hi]) + b_ref[:, lo:hi]

    w = LRU_WIDTH
    g_ref[0] = proj(0, w)
    u_ref[0] = proj(w, 2 * w)
    cos = cos_ref[...]
    sin = sin_ref[...]
    cos4 = jnp.concatenate([cos] * 4, axis=1)
    sin4 = jnp.concatenate([sin] * 4, axis=1)
    o = 2 * w
    q = (proj(o, o + 512) * cos4 + proj(o + 512, o + 1024) * sin4) * (SWA_HEAD_DIM ** -0.5)
    q_ref[0] = q.astype(BF16)
    o += 1024
    cos2 = jnp.concatenate([cos] * 2, axis=1)
    sin2 = jnp.concatenate([sin] * 2, axis=1)
    k = proj(o, o + 256) * cos2 + proj(o + 256, o + 512) * sin2
    k_ref[0] = k.astype(BF16)
    o += 512
    v_ref[0] = proj(o, o + 256).astype(BF16)


def _in_even(x, ctx, mods, w, b, cos, sin):
    bsz, l_lat, d = x.shape
    s_all = l_lat + ctx.shape[1]
    tm = ROW_TILE
    n_lat = l_lat // tm
    nw = w.shape[1]
    row = lambda n: pl.BlockSpec((1, tm, n), lambda bb, i: (bb, i, 0))
    return pl.pallas_call(
        functools.partial(_in_even_kernel, l_lat=l_lat),
        grid=(bsz, s_all // tm),
        in_specs=[pl.BlockSpec((1, tm, d), lambda bb, i: (bb, jnp.minimum(i, n_lat - 1), 0)),
                  pl.BlockSpec((1, tm, d), lambda bb, i: (bb, 0, 0)),
                  pl.BlockSpec((1, 6, d), lambda bb, i: (bb, 0, 0)),
                  pl.BlockSpec((1, 6, d), lambda bb, i: (4, 0, 0)),
                  pl.BlockSpec((d, nw), lambda bb, i: (0, 0)),
                  pl.BlockSpec((1, nw), lambda bb, i: (0, 0)),
                  pl.BlockSpec((tm, LANES), lambda bb, i: (i, 0)),
                  pl.BlockSpec((tm, LANES), lambda bb, i: (i, 0))],
        out_specs=[row(512), row(512), row(512), row(256), row(256)],
        out_shape=[jax.ShapeDtypeStruct((bsz, s_all, 512), F32),
                   jax.ShapeDtypeStruct((bsz, s_all, 512), F32),
                   jax.ShapeDtypeStruct((bsz, s_all, 512), BF16),
                   jax.ShapeDtypeStruct((bsz, s_all, 256), BF16),
                   jax.ShapeDtypeStruct((bsz, s_all, 256), BF16)],
        compiler_params=_cparams(2),
        name="in_even",
    )(x, ctx, mods, mods, w, b, cos, sin)


def _softplus(x):
    return jnp.maximum(x, 0.0) + jnp.log1p(jnp.exp(-jnp.abs(x)))


def _lru_chunk_index(s, n_lat, reverse):
    if reverse:
        return jnp.where(s == 0, n_lat, n_lat - s)
    return jnp.where(s == 0, n_lat, s - 1)


def _lru_kernel(*refs, reverse, n_lat):
    if reverse:
        (u_ref, up_ref, un_ref, cw_ref, cb_ref, w_ref, bias_ref, lam_ref, g_ref, hf_ref,
         o_ref, ext, a_s, b_s, p_s, h_s, carry) = refs
    else:
        (u_ref, up_ref, un_ref, cw_ref, cb_ref, w_ref, bias_ref, lam_ref,
         o_ref, ext, a_s, b_s, p_s, h_s, carry) = refs
    s = pl.program_id(1)
    j = _lru_chunk_index(s, n_lat, reverse)
    t = ROW_TILE
    width = LRU_WIDTH

    @pl.when(s == 0)
    def _():
        carry[...] = jnp.zeros_like(carry)

    prev_ok = jnp.logical_and(j != 0, j != n_lat)
    next_ok = jnp.logical_and(j != n_lat - 1, j != n_lat)
    ext[0:SUBLANES, :] = jnp.where(prev_ok, up_ref[0], 0.0)
    ext[SUBLANES:SUBLANES + t, :] = u_ref[0]
    ext[SUBLANES + t:, :] = jnp.where(next_ok, un_ref[0], 0.0)
    uc = cb_ref[...]
    for kk in range(LRU_CONV_W):
        uc = uc + cw_ref[kk:kk + 1, :] * ext[SUBLANES - 1 + kk:SUBLANES - 1 + kk + t, :]

    z = _dot(uc.astype(BF16), w_ref[...]) + bias_ref[...]
    r = jax.nn.sigmoid(z[:, :width])
    ig = jax.nn.sigmoid(z[:, width:])
    log_a = (-LRU_C) * r * _softplus(-lam_ref[...])
    a = jnp.exp(log_a)
    b = jnp.sqrt(-jnp.tanh(log_a) * (a * a + 1.0)) * (ig * uc)

    steps = range(SCAN_SEG - 1, -1, -1) if reverse else range(SCAN_SEG)
    segs = range(SUBLANES - 1, -1, -1) if reverse else range(SUBLANES)
    blocks = []
    for cb in range(width // LANES):
        cols = slice(cb * LANES, (cb + 1) * LANES)
        for sg in range(SUBLANES):
            a_s[cb, sg * SCAN_PITCH:sg * SCAN_PITCH + SCAN_SEG, :] = a[sg * SCAN_SEG:(sg + 1) * SCAN_SEG, cols]
            b_s[cb, sg * SCAN_PITCH:sg * SCAN_PITCH + SCAN_SEG, :] = b[sg * SCAN_SEG:(sg + 1) * SCAN_SEG, cols]
        h = jnp.zeros((SUBLANES, LANES), F32)
        p = jnp.ones((SUBLANES, LANES), F32)
        for i in steps:
            sl = pl.ds(i, SUBLANES, stride=SCAN_PITCH)
            ai = a_s[cb, sl, :]
            h = ai * h + b_s[cb, sl, :]
            p = p * ai
            h_s[cb, sl, :] = h
            p_s[cb, sl, :] = p
        c = carry[:, cols]
        seg_in = [None] * SUBLANES
        for sg in segs:
            seg_in[sg] = c
            c = p[sg:sg + 1, :] * c + h[sg:sg + 1, :]
        carry[:, cols] = c
        cin = jnp.concatenate(seg_in, axis=0)
        for i in range(SCAN_SEG):
            sl = pl.ds(i, SUBLANES, stride=SCAN_PITCH)
            h_s[cb, sl, :] = h_s[cb, sl, :] + p_s[cb, sl, :] * cin
        blocks.append(jnp.concatenate([h_s[cb, sg * SCAN_PITCH:sg * SCAN_PITCH + SCAN_SEG, :]
                                       for sg in range(SUBLANES)], axis=0))
    hs = jnp.concatenate(blocks, axis=1)
    if reverse:
        o_ref[0] = (_gelu_tanh(g_ref[0]) * (hf_ref[0] + hs)).astype(o_ref.dtype)
    else:
        o_ref[0] = hs


def _lru(u, g, hf, conv_w, conv_b, w, bias, lam, reverse, l_lat):
    bsz, s_all, width = u.shape
    t = ROW_TILE
    n_chunks = s_all // t
    n_lat = l_lat // t
    hb = t // SUBLANES
    n_hb = s_all // SUBLANES
    cidx = functools.partial(_lru_chunk_index, n_lat=n_lat, reverse=reverse)
    chunk = pl.BlockSpec((1, t, width), lambda bb, s: (bb, cidx(s), 0))
    const = lambda shape: pl.BlockSpec(shape, lambda bb, s: (0,) * len(shape))
    in_specs = [chunk,
                pl.BlockSpec((1, SUBLANES, width), lambda bb, s: (bb, jnp.maximum(cidx(s) * hb - 1, 0), 0)),
                pl.BlockSpec((1, SUBLANES, width), lambda bb, s: (bb, jnp.minimum(cidx(s) * hb + hb, n_hb - 1), 0)),
                const((LRU_CONV_W, width)), const((1, width)), const((width, 2 * width)),
                const((1, 2 * width)), const((1, width))]
    args = [u, u, u, conv_w, conv_b, w, bias, lam]
    if reverse:
        in_specs += [chunk, chunk]
        args += [g, hf]
    return pl.pallas_call(
        functools.partial(_lru_kernel, reverse=reverse, n_lat=n_lat),
        grid=(bsz, n_chunks),
        in_specs=in_specs,
        out_specs=chunk,
        out_shape=jax.ShapeDtypeStruct((bsz, s_all, width), BF16 if reverse else F32),
        scratch_shapes=[pltpu.VMEM((t + 2 * SUBLANES, width), F32)]
                       + [pltpu.VMEM((width // LANES, SUBLANES * SCAN_PITCH, LANES), F32)] * 4
                       + [pltpu.VMEM((1, width), F32)],
        compiler_params=_cparams(2),
        name="lru_bwd" if reverse else "lru_fwd",
    )(*args)


def _swa_kernel(sink_ref, q_ref, k_ref, v_ref, o_ref, *, l_lat, n_ctx):
    i = pl.program_id(1)
    tq = q_ref.shape[1]
    wk = tq + 2 * WINDOW
    is_lat = i < l_lat // tq
    ws = pl.multiple_of(jnp.clip(i * tq - WINDOW, 0, l_lat - wk), WINDOW)
    qpos = i * tq + lax.broadcasted_iota(jnp.int32, (tq, 1), 0)
    kpos = ws + lax.broadcasted_iota(jnp.int32, (1, wk), 1)
    band = jnp.logical_and(jnp.abs(qpos - kpos) <= WINDOW, is_lat)
    lo = lax.broadcasted_iota(jnp.int32, (1, LANES), 1) < SWA_HEAD_DIM
    heads_per_kv = SWA_HEADS // SWA_KV_HEADS
    for kh in range(SWA_KV_HEADS):
        cols = slice(LANES * kh, LANES * (kh + 1))
        kw = k_ref[0, pl.ds(ws, wk), cols]
        vw = v_ref[0, pl.ds(ws, wk), cols]
        kc = k_ref[0, l_lat:l_lat + n_ctx, cols]
        vc = v_ref[0, l_lat:l_lat + n_ctx, cols]
        for jq in range(heads_per_kv // 2):
            blk = kh * (heads_per_kv // 2) + jq
            qb = q_ref[0, :, LANES * blk:LANES * (blk + 1)]
            halves = []
            for half in range(2):
                qm = jnp.where(lo if half == 0 else jnp.logical_not(lo), qb, jnp.zeros_like(qb))
                sw = jnp.where(band, _dot_nt(qm, kw), NEG_INF)
                sc = _dot_nt(qm, kc)
                sink = sink_ref[2 * blk + half]
                m = jnp.maximum(jnp.maximum(jnp.max(sw, axis=-1, keepdims=True),
                                            jnp.max(sc, axis=-1, keepdims=True)), sink)
                pw = jnp.exp(sw - m)
                pc = jnp.exp(sc - m)
                den = (jnp.sum(pw, axis=-1, keepdims=True) + jnp.sum(pc, axis=-1, keepdims=True)
                       + jnp.exp(sink - m))
                halves.append((_dot(pw.astype(BF16), vw) + _dot(pc.astype(BF16), vc)) / den)
            o_ref[0, :, LANES * blk:LANES * (blk + 1)] = jnp.where(lo, halves[0], halves[1]).astype(o_ref.dtype)


def _swa(q, k2, v2, sink, l_lat):
    bsz, s_all, _ = q.shape
    tq = ROW_TILE
    n_ctx = s_all - l_lat
    return pl.pallas_call(
        functools.partial(_swa_kernel, l_lat=l_lat, n_ctx=n_ctx),
        grid_spec=pltpu.PrefetchScalarGridSpec(
            num_scalar_prefetch=1,
            grid=(bsz, s_all // tq),
            in_specs=[pl.BlockSpec((1, tq, 512), lambda bb, i, s: (bb, i, 0)),
                      pl.BlockSpec((1, s_all, 256), lambda bb, i, s: (bb, 0, 0)),
                      pl.BlockSpec((1, s_all, 256), lambda bb, i, s: (bb, 0, 0))],
            out_specs=pl.BlockSpec((1, tq, 512), lambda bb, i, s: (bb, i, 0))),
        out_shape=jax.ShapeDtypeStruct((bsz, s_all, 512), BF16),
        compiler_params=_cparams(2),
        name="swa",
    )(sink, q, k2, v2)


def _route(logits):
    lane = lax.broadcasted_iota(jnp.int32, logits.shape, 1)
    big = jnp.int32(2 ** 30)

    def first_argmax(vals, mask):
        v = jnp.where(mask, vals, NEG_INF)
        m = jnp.max(v, axis=-1, keepdims=True)
        idx = jnp.min(jnp.where(jnp.logical_and(mask, v == m), lane, big), axis=-1, keepdims=True)
        return m, idx

    gmask = lane < N_GROUPS
    gm, gidx = first_argmax(logits, gmask)
    g_prob = 1.0 / jnp.sum(jnp.where(gmask, jnp.exp(logits - gm), 0.0), axis=-1, keepdims=True)
    base = N_GROUPS + gidx * EXPERTS_PER_GROUP
    emask = jnp.logical_and(lane >= base, lane < base + EXPERTS_PER_GROUP)
    m0, i0 = first_argmax(logits, emask)
    m1, i1 = first_argmax(logits, jnp.logical_and(emask, lane != i0))
    w1 = jnp.exp(m1 - m0)
    inv = g_prob / (1.0 + w1)
    return i0 - N_GROUPS, i1 - N_GROUPS, inv, w1 * inv


TOKEN_SUB = D_MODEL // LANES


def _store_token_rows(ref, val):
    n = val.shape[0]
    for j in range(TOKEN_SUB):
        ref[pl.ds(j, n, stride=TOKEN_SUB), :] = val[:, j * LANES:(j + 1) * LANES]


def _load_token_rows(ref, n):
    return jnp.concatenate([ref[pl.ds(j, n, stride=TOKEN_SUB), :] for j in range(TOKEN_SUB)], axis=1)


def _post_mix_kernel(a_ref, b_ref, x_ref, c_ref, mb_ref, mc_ref, w_ref, bo_ref, lng_ref, lnb_ref, wr_ref, br_ref,
                     x1_ref, f_ref, r_ref, cnt_ref, counts, *, l_lat):
    i = pl.program_id(1)
    tm = x_ref.shape[1]
    rows = i * tm + lax.broadcasted_iota(jnp.int32, (tm, 1), 0)
    is_ctx = rows >= l_lat
    half = a_ref.shape[2]
    y = _dot(a_ref[0], w_ref[0:half, :]) + _dot(b_ref[0], w_ref[half:, :]) + bo_ref[...]
    gate = _select_mod(mb_ref, mc_ref, 2, is_ctx)
    x_in = jnp.where(i * tm >= l_lat, c_ref[0], x_ref[0])
    x1 = _layer_norm(DN_ALPHA * x_in + gate * y, lng_ref[...], lnb_ref[...])
    x1_ref[0] = x1
    f = x1 * (1.0 + _select_mod(mb_ref, mc_ref, 4, is_ctx)) + _select_mod(mb_ref, mc_ref, 3, is_ctx)
    _store_token_rows(f_ref.at[0], f)
    f_hi, f_lo = _split_bf16(f)
    two = _dot(f_hi, wr_ref[...])
    logits = two[:, :LANES] + two[:, LANES:] + _dot(f_lo, wr_ref[:, :LANES]) + br_ref[...]
    e0, e1, g0, g1 = _route(logits)

    @pl.when(jnp.logical_and(pl.program_id(0) == 0, i == 0))
    def _():
        counts[...] = jnp.zeros_like(counts)

    lane = lax.broadcasted_iota(jnp.int32, (tm, LANES), 1)
    hot0 = lane == e0
    hot1 = lane == e1
    both = jnp.where(hot0, 1.0, 0.0) + jnp.where(hot1, 1.0, 0.0)
    tri = (lax.broadcasted_iota(jnp.int32, (tm, tm), 1) < lax.broadcasted_iota(jnp.int32, (tm, tm), 0))
    before = _dot(jnp.where(tri, 1.0, 0.0).astype(BF16), both.astype(BF16)) + counts[...]
    r0 = jnp.sum(jnp.where(hot0, before, 0.0), axis=-1, keepdims=True)
    r1 = jnp.sum(jnp.where(hot1, before, 0.0), axis=-1, keepdims=True)
    counts[...] = counts[...] + jnp.sum(both, axis=0, keepdims=True)
    cnt_ref[...] = counts[...]
    r_ref[0] = jnp.where(lane == 0, e0.astype(F32), jnp.where(lane == 1, e1.astype(F32), jnp.where(
        lane == 2, g0, jnp.where(lane == 3, g1, jnp.where(lane == 4, r0, jnp.where(lane == 5, r1, 0.0))))))


def _post_mix(a, b, x, ctx, mods, w_out, b_out, ln_g, ln_b, w_r, b_r, n_rows, l_lat):
    bsz = x.shape[0]
    d = x.shape[2]
    tm = ROW_TILE
    half = a.shape[2]
    n_lat = l_lat // tm
    row = lambda n: pl.BlockSpec((1, tm, n), lambda bb, i: (bb, i, 0))
    const = lambda shape: pl.BlockSpec(shape, lambda bb, i: (0,) * len(shape))
    return pl.pallas_call(
        functools.partial(_post_mix_kernel, l_lat=l_lat),
        grid=(bsz, n_rows // tm),
        in_specs=[row(half), row(half),
                  pl.BlockSpec((1, tm, d), lambda bb, i: (bb, jnp.minimum(i, n_lat - 1), 0)),
                  pl.BlockSpec((1, tm, d), lambda bb, i: (bb, 0, 0)),
                  pl.BlockSpec((1, 6, d), lambda bb, i: (bb, 0, 0)),
                  pl.BlockSpec((1, 6, d), lambda bb, i: (4, 0, 0)),
                  const((2 * half, d)), const((1, d)), const((1, d)), const((1, d)),
                  const((d, 2 * LANES)), const((1, LANES))],
        out_specs=[row(d), pl.BlockSpec((1, tm * TOKEN_SUB, LANES), lambda bb, i: (bb, i, 0)), row(LANES),
                   const((1, LANES))],
        out_shape=[jax.ShapeDtypeStruct((bsz, n_rows, d), F32),
                   jax.ShapeDtypeStruct((bsz, n_rows * TOKEN_SUB, LANES), F32),
                   jax.ShapeDtypeStruct((bsz, n_rows, LANES), F32),
                   jax.ShapeDtypeStruct((1, LANES), F32)],
        scratch_shapes=[pltpu.VMEM((1, LANES), F32)],
        compiler_params=_cparams(2),
        name="post_mix",
    )(a, b, x, ctx, mods, mods, w_out, b_out, ln_g, ln_b, w_r, b_r)


def _token_rows(ref, tok, n=1):
    start = tok * TOKEN_SUB
    if not isinstance(start, int):
        start = pl.multiple_of(start, TOKEN_SUB)
    return ref.at[pl.ds(start, n * TOKEN_SUB), :]


def _dispatch_kernel(pad_ref, nb_ref, dest_ref, f_ref, xb_ref, zeros, sem_z, sem):
    i = pl.program_id(0)
    tm = f_ref.shape[0] // TOKEN_SUB
    tb = zeros.shape[0] // TOKEN_SUB
    n_blocks = xb_ref.shape[0] // zeros.shape[0]

    @pl.when(i == 0)
    def _():
        zeros[...] = jnp.zeros_like(zeros)
        fills = [pltpu.make_async_copy(zeros, _token_rows(xb_ref, pad_ref[e], tb), sem_z) for e in range(N_EXPERTS)]
        for cp in fills:
            cp.start()
        for cp in fills:
            cp.wait()

        def tail(b):
            return pltpu.make_async_copy(zeros, _token_rows(xb_ref, b * tb, tb), sem_z)

        def start(b, carry):
            tail(b).start()
            return carry

        def wait(b, carry):
            tail(b).wait()
            return carry

        lax.fori_loop(nb_ref[0], n_blocks, start, 0)
        lax.fori_loop(nb_ref[0], n_blocks, wait, 0)

    copies = [pltpu.make_async_copy(_token_rows(f_ref, r), _token_rows(xb_ref, dest_ref[0, 0, kk * tm + r]), sem)
              for r in range(tm) for kk in range(TOP_K)]
    for n, cp in enumerate(copies):
        cp.start(priority=n % 2)
    for cp in copies:
        cp.wait()


def _dispatch(pad_start, n_used, dest_tiles, tokens, n_slots):
    n_tok = tokens.shape[0] // TOKEN_SUB
    tm = ROW_TILE
    tb = MOE_TILE
    return pl.pallas_call(
        _dispatch_kernel,
        grid_spec=pltpu.PrefetchScalarGridSpec(
            num_scalar_prefetch=2,
            grid=(n_tok // tm,),
            in_specs=[pl.BlockSpec((1, 1, TOP_K * tm), lambda i, pad, nb: (i, 0, 0), memory_space=pltpu.SMEM),
                      pl.BlockSpec((tm * TOKEN_SUB, LANES), lambda i, pad, nb: (i, 0))],
            out_specs=pl.BlockSpec(memory_space=pl.ANY),
            scratch_shapes=[pltpu.VMEM((tb * TOKEN_SUB, LANES), F32), pltpu.SemaphoreType.DMA,
                            pltpu.SemaphoreType.DMA]),
        out_shape=jax.ShapeDtypeStruct(((n_slots + tb) * TOKEN_SUB, LANES), F32),
        compiler_params=pltpu.CompilerParams(dimension_semantics=("arbitrary",), vmem_limit_bytes=VMEM_LIMIT,
                                             disable_bounds_checks=True),
        name="moe_dispatch",
    )(pad_start, n_used, dest_tiles, tokens)


def _expert_kernel(be_ref, nb_ref, x_ref, w1_ref, w3_ref, w2_ref, o_ref, w1b, w3b, w2b):
    i = pl.program_id(0)
    prev = be_ref[jnp.maximum(i - 1, 0)]

    @pl.when(jnp.logical_or(i == 0, be_ref[i] != prev))
    def _():
        w1b[...] = w1_ref[0].astype(BF16)
        w3b[...] = w3_ref[0].astype(BF16)
        w2b[...] = w2_ref[0].astype(BF16)

    @pl.when(i < nb_ref[0])
    def _():
        xb = _load_token_rows(x_ref, x_ref.shape[0] // TOKEN_SUB).astype(BF16)
        h1 = _dot(xb, w1b[...])
        h3 = _dot(xb, w3b[...])
        act = (_silu(h1) * h3).astype(BF16)
        _store_token_rows(o_ref, _dot(act, w2b[...]))

    @pl.when(i >= nb_ref[0])
    def _():
        o_ref[...] = jnp.zeros_like(o_ref)


def _experts(block_expert, n_used, xb, w1, w3, w2):
    tb = MOE_TILE
    n_blocks = block_expert.shape[0]
    _, d, de = w1.shape
    last = lambda i, nb: jnp.minimum(i, jnp.maximum(nb[0] - 1, 0))
    return pl.pallas_call(
        _expert_kernel,
        grid_spec=pltpu.PrefetchScalarGridSpec(
            num_scalar_prefetch=2,
            grid=(n_blocks,),
            in_specs=[pl.BlockSpec((tb * TOKEN_SUB, LANES), lambda i, be, nb: (last(i, nb), 0)),
                      pl.BlockSpec((1, d, de), lambda i, be, nb: (be[i], 0, 0)),
                      pl.BlockSpec((1, d, de), lambda i, be, nb: (be[i], 0, 0)),
                      pl.BlockSpec((1, de, d), lambda i, be, nb: (be[i], 0, 0))],
            out_specs=pl.BlockSpec((tb * TOKEN_SUB, LANES), lambda i, be, nb: (i, 0)),
            scratch_shapes=[pltpu.VMEM((d, de), BF16), pltpu.VMEM((d, de), BF16), pltpu.VMEM((de, d), BF16)]),
        out_shape=jax.ShapeDtypeStruct((n_blocks * tb * TOKEN_SUB, LANES), F32),
        compiler_params=_cparams(1),
        name="experts",
    )(block_expert, n_used, xb, w1, w3, w2)


def _post_ffn_kernel(dcur_ref, dnxt_ref, x_ref, r_ref, yb_ref, mb_ref, mc_ref, lng_ref, lnb_ref, o_ref, ybuf, sems,
                     *, l_lat):
    bb, i = pl.program_id(0), pl.program_id(1)
    nt = pl.num_programs(1)
    step = bb * nt + i
    n_steps = pl.num_programs(0) * nt
    tm = x_ref.shape[1]

    def gathers(dref, slot):
        return [pltpu.make_async_copy(_token_rows(yb_ref, dref[0, 0, kk * tm + r]),
                                      _token_rows(ybuf.at[slot, kk], r), sems.at[slot])
                for r in range(tm) for kk in range(TOP_K)]

    def issue(dref, slot):
        for n, cp in enumerate(gathers(dref, slot)):
            cp.start(priority=n % 2)

    cur = step % 2

    @pl.when(step == 0)
    def _():
        issue(dcur_ref, 0)

    @pl.when(step + 1 < n_steps)
    def _():
        issue(dnxt_ref, 1 - cur)

    for cp in gathers(dcur_ref, cur):
        cp.wait()

    rows = i * tm + lax.broadcasted_iota(jnp.int32, (tm, 1), 0)
    is_ctx = rows >= l_lat
    gate = _select_mod(mb_ref, mc_ref, 5, is_ctx)
    r = r_ref[0]
    ffn = r[:, 2:3] * _load_token_rows(ybuf.at[cur, 0], tm) + r[:, 3:4] * _load_token_rows(ybuf.at[cur, 1], tm)
    o_ref[0] = _layer_norm(DN_ALPHA * x_ref[0] + gate * ffn, lng_ref[...], lnb_ref[...])


def _post_ffn(x1, route, dest_tiles, yb, mods, ln_g, ln_b, l_lat):
    bsz, n_rows, d = x1.shape
    tm = ROW_TILE
    nt = n_rows // tm
    n_steps = bsz * nt
    row = lambda n: pl.BlockSpec((1, tm, n), lambda bb, i: (bb, i, 0))
    const = lambda shape: pl.BlockSpec(shape, lambda bb, i: (0,) * len(shape))
    dspec = lambda ahead: pl.BlockSpec((1, 1, TOP_K * tm),
                                       lambda bb, i: (jnp.minimum(bb * nt + i + ahead, n_steps - 1), 0, 0),
                                       memory_space=pltpu.SMEM)
    return pl.pallas_call(
        functools.partial(_post_ffn_kernel, l_lat=l_lat),
        grid=(bsz, nt),
        in_specs=[dspec(0), dspec(1), row(d), row(LANES), pl.BlockSpec(memory_space=pl.ANY),
                  pl.BlockSpec((1, 6, d), lambda bb, i: (bb, 0, 0)),
                  pl.BlockSpec((1, 6, d), lambda bb, i: (4, 0, 0)),
                  const((1, d)), const((1, d))],
        out_specs=row(d),
        out_shape=jax.ShapeDtypeStruct((bsz, n_rows, d), F32),
        scratch_shapes=[pltpu.VMEM((2, TOP_K, tm * TOKEN_SUB, LANES), F32), pltpu.SemaphoreType.DMA((2,))],
        compiler_params=pltpu.CompilerParams(dimension_semantics=("arbitrary", "arbitrary"),
                                             vmem_limit_bytes=VMEM_LIMIT, disable_bounds_checks=True),
        name="post_ffn",
    )(dest_tiles, dest_tiles, x1, route, yb, mods, mods, ln_g, ln_b)


def _moe(f, route, counts, w1, w3, w2):
    bsz, n_rows, _ = route.shape
    n_tok = bsz * n_rows
    tb = MOE_TILE
    tm = ROW_TILE
    r = route.reshape(n_tok, LANES)
    expert = r[:, 0:TOP_K].astype(jnp.int32)
    rank = r[:, 2 * TOP_K:3 * TOP_K].astype(jnp.int32)
    cnt = counts[0, :N_EXPERTS].astype(jnp.int32)
    padded = (cnt + tb - 1) // tb * tb
    p_ends = jnp.cumsum(padded)
    p_starts = p_ends - padded
    ids = jnp.arange(N_EXPERTS, dtype=jnp.int32)
    dest = jnp.sum(jnp.where(expert[:, :, None] == ids[None, None, :], p_starts[None, None, :], 0), axis=-1) + rank
    dest_tiles = dest.reshape(n_tok // tm, tm, TOP_K).transpose(0, 2, 1).reshape(n_tok // tm, 1, TOP_K * tm)
    n_blocks = -(-(n_tok * TOP_K + N_EXPERTS * (tb - 1)) // tb)
    block_start = jnp.arange(n_blocks, dtype=jnp.int32) * tb
    block_expert = jnp.minimum(jnp.sum((p_ends[None, :] <= block_start[:, None]).astype(jnp.int32), axis=1),
                               N_EXPERTS - 1)
    n_used = (p_ends[-1] // tb).reshape(1)
    xb = _dispatch(p_starts + cnt, n_used, dest_tiles, f.reshape(n_tok * TOKEN_SUB, LANES), n_blocks * tb)
    return _experts(block_expert, n_used, xb, w1, w3, w2), dest_tiles


def _rms(x, g):
    return x * lax.rsqrt(jnp.mean(x * x, axis=-1, keepdims=True) + LN_EPS) * g

def _in_odd_kernel(x_ref, mb_ref, mc_ref, w_ref, b_ref, qn_ref, kvn_ref, wuq_ref, wuk_ref, place_ref, wuvt_ref,
                   cq_ref, sq_ref, ck_ref, sk_ref, q_ref, k_ref, vt_ref, glu_ref, *, l_lat):
    i = pl.program_id(0)
    tm = x_ref.shape[1]
    rows = i * tm + lax.broadcasted_iota(jnp.int32, (tm, 1), 0)
    is_ctx = rows >= l_lat
    shift = _select_mod(mb_ref, mc_ref, 0, is_ctx)
    scale = _select_mod(mb_ref, mc_ref, 1, is_ctx)
    h = (x_ref[0] * (1.0 + scale) + shift).astype(BF16)

    def proj(lo, hi):
        return _dot(h, w_ref[:, lo:hi]) + b_ref[:, lo:hi]

    cqn = _rms(proj(0, MLA_Q_RANK), qn_ref[...]).astype(BF16)
    hw = MLA_HEADS * LANES
    cos8 = jnp.concatenate([cq_ref[...]] * MLA_HEADS, axis=1)
    sin8 = jnp.concatenate([sq_ref[...]] * MLA_HEADS, axis=1)
    q = _dot(cqn, wuq_ref[:, :hw]) * cos8 + _dot(cqn, wuq_ref[:, hw:]) * sin8
    q_ref[0] = (q * ((MLA_NOPE + MLA_ROPE) ** -0.5 * LOG2E)).astype(BF16)
    o = MLA_Q_RANK
    cn = _rms(proj(o, o + MLA_KV_RANK), kvn_ref[...]).astype(BF16)
    o += MLA_KV_RANK
    kpe = proj(o, o + LANES) * ck_ref[...] + proj(o + LANES, o + 2 * LANES) * sk_ref[...]
    k_ref[0] = (_dot(cn, wuk_ref[...]) + _dot(kpe.astype(BF16), place_ref[...])).astype(BF16)
    vt = _dot_nt(wuvt_ref[...], cn)
    vt_ref[0] = vt.reshape(MLA_HEADS, 1, MLA_V, tm).astype(BF16)
    o += 2 * LANES
    glu_ref[0] = proj(o, o + CONF_CH) * jax.nn.sigmoid(proj(o + CONF_CH, o + 2 * CONF_CH))


def _in_odd(x_all, mods, w, b, qn, kvn, wuq, wuk, place, wuvt, cq, sq, ck, sk, l_lat, tm):
    bsz, s_all, d = x_all.shape
    nw = w.shape[1]
    nk = s_all // tm
    hw = MLA_HEADS * LANES
    row = lambda n: pl.BlockSpec((1, tm, n), lambda i, bb: (bb, i, 0))
    const = lambda shape: pl.BlockSpec(shape, lambda i, bb: (0,) * len(shape))
    tab = pl.BlockSpec((tm, LANES), lambda i, bb: (i, 0))
    return pl.pallas_call(
        functools.partial(_in_odd_kernel, l_lat=l_lat),
        grid=(nk, bsz),
        in_specs=[row(d),
                  pl.BlockSpec((1, 6, d), lambda i, bb: (bb, 0, 0)),
                  pl.BlockSpec((1, 6, d), lambda i, bb: (4, 0, 0)),
                  const((d, nw)), const((1, nw)), const((1, MLA_Q_RANK)), const((1, MLA_KV_RANK)),
                  const((MLA_Q_RANK, 2 * hw)), const((MLA_KV_RANK, hw)), const((LANES, hw)),
                  const((MLA_HEADS * MLA_V, MLA_KV_RANK)), tab, tab, tab, tab],
        out_specs=[row(hw), row(hw),
                   pl.BlockSpec((1, MLA_HEADS, 1, MLA_V, tm), lambda i, bb: (bb, 0, i, 0, 0)),
                   row(CONF_CH)],
        out_shape=[jax.ShapeDtypeStruct((bsz, s_all, hw), BF16),
                   jax.ShapeDtypeStruct((bsz, s_all, hw), BF16),
                   jax.ShapeDtypeStruct((bsz, MLA_HEADS, nk, MLA_V, tm), BF16),
                   jax.ShapeDtypeStruct((bsz, s_all, CONF_CH), F32)],
        compiler_params=_cparams(2),
        name="in_odd",
    )(x_all, mods, mods, w, b, qn, kvn, wuq, wuk, place, wuvt, cq, sq, ck, sk)


def _flash_kernel(q_ref, k_ref, vt_ref, o_ref, s_scr, *, tk, nk):
    tq = q_ref.shape[1]
    chains = [(hh, j) for hh in range(2) for j in range(tq // FLASH_SUB)]

    def scores(c, slot):
        off = c * tk if isinstance(c, int) else pl.multiple_of(c * tk, tk)
        for ci, (hh, j) in enumerate(chains):
            q = q_ref[0, j * FLASH_SUB:(j + 1) * FLASH_SUB, LANES * hh:LANES * (hh + 1)]
            k = k_ref[0, pl.ds(off, tk), LANES * hh:LANES * (hh + 1)]
            s_scr[slot, ci] = _dot_nt(k, q)

    def update(c, slot, carry):
        probs = []
        for ci, (m, l, acc) in enumerate(carry):
            m_new = jnp.maximum(m, jnp.max(s_scr[slot, ci], axis=0, keepdims=True))
            alpha = jnp.exp2(m - m_new)
            p = jnp.exp2(s_scr[slot, ci] - m_new)
            probs.append((m_new, alpha, alpha * l + jnp.sum(p, axis=0, keepdims=True), p.astype(BF16)))
        out = []
        for (hh, j), (m_new, alpha, l, p), (_, _, acc) in zip(chains, probs, carry):
            out.append((m_new, l, alpha * acc + _dot(vt_ref[0, hh, c], p)))
        return tuple(out)

    def pair(i, carry):
        c = 2 * i
        scores(c + 1, 1)
        carry = update(c, 0, carry)
        scores(c + 2, 0)
        return update(c + 1, 1, carry)

    init = tuple((jnp.full((1, FLASH_SUB), NEG_INF, F32), jnp.zeros((1, FLASH_SUB), F32),
                  jnp.zeros((MLA_V, FLASH_SUB), F32)) for _ in chains)
    scores(0, 0)
    n_pairs = (nk - 1) // 2
    res = lax.fori_loop(0, n_pairs, pair, init)
    if nk - 2 * n_pairs == 2:
        scores(nk - 1, 1)
        res = update(nk - 2, 0, res)
        res = update(nk - 1, 1, res)
    else:
        res = update(nk - 1, 0, res)
    norm = {ch: acc / l for ch, (m, l, acc) in zip(chains, res)}
    for j in range(tq // FLASH_SUB):
        pair = jnp.concatenate([norm[(0, j)], norm[(1, j)]], axis=0)
        o_ref[0, j * FLASH_SUB:(j + 1) * FLASH_SUB, :] = pair.T.astype(o_ref.dtype)


def _flash(q, k, vt, l_lat, tk):
    bsz, s_all, _ = q.shape
    nk = s_all // tk
    tq = FLASH_TQ if l_lat % FLASH_TQ == 0 else FLASH_SUB
    return pl.pallas_call(
        functools.partial(_flash_kernel, tk=tk, nk=nk),
        grid=(bsz, MLA_HEADS // 2, l_lat // tq),
        in_specs=[pl.BlockSpec((1, tq, 2 * LANES), lambda bb, hp, i: (bb, i, hp)),
                  pl.BlockSpec((1, s_all, 2 * LANES), lambda bb, hp, i: (bb, 0, hp)),
                  pl.BlockSpec((1, 2, nk, MLA_V, tk), lambda bb, hp, i: (bb, hp, 0, 0, 0))],
        out_specs=pl.BlockSpec((1, tq, LANES), lambda bb, hp, i: (bb, i, hp)),
        out_shape=jax.ShapeDtypeStruct((bsz, l_lat, MLA_HEADS * MLA_V), BF16),
        scratch_shapes=[pltpu.VMEM((2, 2 * tq // FLASH_SUB, tk, FLASH_SUB), F32)],
        compiler_params=_cparams(3),
        name="mla_attention",
    )(q, k, vt)


def _conf_kernel(x_ref, xp_ref, xn_ref, w_ref, b_ref, g_ref, bb_ref, o_ref, ext, shifted):
    i = pl.program_id(1)
    n = pl.num_programs(1)
    t = x_ref.shape[1]
    hl = CONF_HALO
    ext[0:hl, :] = jnp.where(i > 0, xp_ref[0], 0.0)
    ext[hl:hl + t, :] = x_ref[0]
    ext[hl + t:, :] = jnp.where(i < n - 1, xn_ref[0], 0.0)
    span = shifted.shape[1]
    for s in range(SUBLANES):
        shifted[s] = ext[s:s + span, :]
    off = hl - CONF_K // 2
    for r in range(t // CONF_SUB):
        acc = jnp.zeros((CONF_SUB, CONF_CH), F32) + b_ref[...]
        for kk in range(CONF_K):
            st = off + kk
            base = st // SUBLANES * SUBLANES + r * CONF_SUB
            acc = acc + w_ref[kk:kk + 1, :] * shifted[st % SUBLANES, base:base + CONF_SUB, :]
        o_ref[0, r * CONF_SUB:(r + 1) * CONF_SUB, :] = _silu(_layer_norm(acc, g_ref[...], bb_ref[...])).astype(o_ref.dtype)


def _conformer(glu, w, b, g, beta, l_lat):
    bsz, _, ch = glu.shape
    t = ROW_TILE
    hl = CONF_HALO
    hb = t // hl
    n_hb = l_lat // hl
    const = lambda shape: pl.BlockSpec(shape, lambda bb, i: (0,) * len(shape))
    return pl.pallas_call(
        _conf_kernel,
        grid=(bsz, l_lat // t),
        in_specs=[pl.BlockSpec((1, t, ch), lambda bb, i: (bb, i, 0)),
                  pl.BlockSpec((1, hl, ch), lambda bb, i: (bb, jnp.maximum(i * hb - 1, 0), 0)),
                  pl.BlockSpec((1, hl, ch), lambda bb, i: (bb, jnp.minimum(i * hb + hb, n_hb - 1), 0)),
                  const((CONF_K, ch)), const((1, ch)), const((1, ch)), const((1, ch))],
        out_specs=pl.BlockSpec((1, t, ch), lambda bb, i: (bb, i, 0)),
        out_shape=jax.ShapeDtypeStruct((bsz, l_lat, ch), BF16),
        scratch_shapes=[pltpu.VMEM((t + 2 * hl, ch), F32),
                        pltpu.VMEM((SUBLANES, t + 2 * hl - SUBLANES, ch), F32)],
        compiler_params=_cparams(2),
        name="conformer_conv",
    )(glu, glu, glu, w, b, g, beta)


def _block_diag(w):
    nb, c, d = w.shape
    out = jnp.zeros((nb, c, nb, d), w.dtype)
    out = out.at[jnp.arange(nb), :, jnp.arange(nb), :].set(w)
    return out.reshape(nb * c, nb * d)


def _router_weights(w_group, b_group, w_router, b_router):
    d = w_group.shape[0]
    w = jnp.zeros((d, LANES), F32).at[:, :N_GROUPS].set(w_group).at[:, N_GROUPS:N_GROUPS + N_EXPERTS].set(w_router)
    hi, lo = _split_bf16(w)
    b = jnp.zeros((1, LANES), F32).at[0, :N_GROUPS].set(b_group).at[0, N_GROUPS:N_GROUPS + N_EXPERTS].set(b_router)
    return jnp.concatenate([hi, lo], axis=1), b


def kernel(x, c, ctx, c_ctx, w_mod, b_mod, ln_g, ln_b, e_w_in, e_b_in, e_conv_w, e_conv_b, e_lru_wa, e_lru_ba, e_lru_wx, e_lru_bx, e_lru_lambda, e_sink, e_w_out, e_b_out, o_w_in, o_b_in, o_q_norm, o_kv_norm, o_w_uq, o_w_uk, o_w_uv, o_dw_w, o_dw_b, o_cln_g, o_cln_b, o_w_out, o_b_out, moe_w_group, moe_b_group, moe_w_router, moe_b_router, moe_w1, moe_w3, moe_w2):
    bsz, l_lat, d = x.shape
    n_ctx = ctx.shape[1]
    s_all = l_lat + n_ctx
    assert d == D_MODEL and bsz <= 4 and l_lat % ROW_TILE == 0 and n_ctx == ROW_TILE and l_lat % GRID_W == 0
    assert w_mod.shape[0] == DEPTH == 2

    cond8 =jnp.zeros((SUBLANES, d), F32).at[:bsz].set(c).at[4].set(c_ctx)
    mods = _modulation(cond8, w_mod, b_mod).reshape(DEPTH, SUBLANES, 6, d)

    t_idx = jnp.arange(l_lat, dtype=jnp.int32)
    row_pos, col_pos = t_idx // GRID_W, t_idx % GRID_W

    cos64, sin64, partner64 = _rope_pattern(SWA_HEAD_DIM, row_pos, col_pos)
    cos_e = jnp.ones((s_all, LANES), F32).at[:l_lat].set(jnp.concatenate([cos64] * 2, axis=1))
    sin_e = jnp.zeros((s_all, LANES), F32).at[:l_lat].set(jnp.concatenate([sin64] * 2, axis=1))
    w_in, b_in = e_w_in[0], e_b_in[0]
    oq = 2 * LRU_WIDTH
    ok = oq + SWA_HEADS * SWA_HEAD_DIM
    ov = ok + SWA_KV_HEADS * SWA_HEAD_DIM
    q_cols = oq + np.arange(SWA_HEADS * SWA_HEAD_DIM)
    qp_cols = oq + (np.arange(SWA_HEADS)[:, None] * SWA_HEAD_DIM + partner64[None, :]).reshape(-1)
    dup = np.repeat(np.arange(SWA_KV_HEADS), 2)
    k_cols = ok + (dup[:, None] * SWA_HEAD_DIM + np.arange(SWA_HEAD_DIM)[None, :]).reshape(-1)
    kp_cols = ok + (dup[:, None] * SWA_HEAD_DIM + partner64[None, :]).reshape(-1)
    v_cols = ov + (dup[:, None] * SWA_HEAD_DIM + np.arange(SWA_HEAD_DIM)[None, :]).reshape(-1)
    cols = np.concatenate([np.arange(oq), q_cols, qp_cols, k_cols, kp_cols, v_cols])
    g0, u0, q0, k0, v0 = _in_even(x, ctx, mods[0], w_in[:, cols].astype(BF16), b_in[cols][None, :], cos_e, sin_e)

    def lru_weights(dd):
        w = jnp.concatenate([_block_diag(e_lru_wa[0, dd]), _block_diag(e_lru_wx[0, dd])], axis=1).astype(BF16)
        bias = jnp.concatenate([e_lru_ba[0, dd], e_lru_bx[0, dd]])[None, :]
        return w, bias, e_lru_lambda[0, dd][None, :]

    cw, cb = e_conv_w[0], e_conv_b[0][None, :]
    hf = _lru(u0, None, None, cw, cb, *lru_weights(0), reverse=False, l_lat=l_lat)
    rec = _lru(u0, g0, hf, cw, cb, *lru_weights(1), reverse=True, l_lat=l_lat)
    att = _swa(q0, k0, v0, e_sink[0], l_lat)
    w_r, b_r = _router_weights(moe_w_group[0], moe_b_group[0], moe_w_router[0], moe_b_router[0])
    x1, f, route, counts = _post_mix(rec, att, x, ctx, mods[0], e_w_out[0].astype(BF16), e_b_out[0][None, :],
                                     ln_g[0, 0][None, :], ln_b[0, 0][None, :], w_r, b_r, s_all, l_lat)
    yb, dest = _moe(f, route, counts, moe_w1[0], moe_w3[0], moe_w2[0])
    x2 = _post_ffn(x1, route, dest, yb, mods[0], ln_g[0, 1][None, :], ln_b[0, 1][None, :], l_lat)

    tk = 768 if s_all % 768 == 0 else ROW_TILE
    cos32, sin32, partner32 = _rope_pattern(MLA_ROPE, row_pos, col_pos)
    w_in, b_in = o_w_in[0], o_b_in[0]
    o_pe = MLA_Q_RANK + MLA_KV_RANK
    o_cv = o_pe + MLA_ROPE

    def pad_cols(wm, bv, n):
        return (jnp.zeros((wm.shape[0], n), F32).at[:, :wm.shape[1]].set(wm),
                jnp.zeros((n,), F32).at[:bv.shape[0]].set(bv))

    pe_cols = o_pe + np.arange(MLA_ROPE)
    wa_, ba_ = pad_cols(w_in[:, pe_cols], b_in[pe_cols], LANES)
    wb_, bb_ = pad_cols(w_in[:, o_pe + partner32], b_in[o_pe + partner32], LANES)
    w_odd = jnp.concatenate([w_in[:, :o_pe], wa_, wb_, w_in[:, o_cv:]], axis=1).astype(BF16)
    b_odd = jnp.concatenate([b_in[:o_pe], ba_, bb_, b_in[o_cv:]])[None, :]
    ck = jnp.ones((s_all, LANES), F32).at[:l_lat, :MLA_ROPE].set(cos32)
    sk = jnp.zeros((s_all, LANES), F32).at[:l_lat, :MLA_ROPE].set(sin32)
    cq = jnp.ones((s_all, LANES), F32).at[:l_lat, MLA_NOPE:MLA_NOPE + MLA_ROPE].set(cos32)
    sq = jnp.zeros((s_all, LANES), F32).at[:l_lat, MLA_NOPE:MLA_NOPE + MLA_ROPE].set(sin32)
    hd = MLA_NOPE + MLA_ROPE
    wuq = o_w_uq[0].reshape(MLA_Q_RANK, MLA_HEADS, hd)
    wuq_main = jnp.zeros((MLA_Q_RANK, MLA_HEADS, LANES), F32).at[:, :, :hd].set(wuq)
    wuq_part = jnp.zeros((MLA_Q_RANK, MLA_HEADS, LANES), F32).at[:, :, MLA_NOPE:hd].set(wuq[:, :, MLA_NOPE + partner32])
    hw = MLA_HEADS * LANES
    wuq_ext = jnp.concatenate([wuq_main.reshape(MLA_Q_RANK, hw), wuq_part.reshape(MLA_Q_RANK, hw)], axis=1).astype(BF16)
    wuk = jnp.zeros((MLA_KV_RANK, MLA_HEADS, LANES), F32).at[:, :, :MLA_NOPE].set(
        o_w_uk[0].reshape(MLA_KV_RANK, MLA_HEADS, MLA_NOPE)).reshape(MLA_KV_RANK, hw).astype(BF16)
    place = np.zeros((LANES, MLA_HEADS, LANES), np.float32)
    for hh in range(MLA_HEADS):
        place[np.arange(MLA_ROPE), hh, MLA_NOPE + np.arange(MLA_ROPE)] = 1.0
    place = jnp.asarray(place.reshape(LANES, hw), BF16)
    wuvt = o_w_uv[0].T.astype(BF16)
    q1, k1, vt1, glu = _in_odd(x2, mods[1], w_odd, b_odd, o_q_norm[0][None, :], o_kv_norm[0][None, :], wuq_ext, wuk,
                               place, wuvt, cq, sq, ck, sk, l_lat, tk)
    att1 = _flash(q1, k1, vt1, l_lat, tk)
    conv1 = _conformer(glu, o_dw_w[0], o_dw_b[0][None, :], o_cln_g[0][None, :], o_cln_b[0][None, :], l_lat)
    w_r, b_r = _router_weights(moe_w_group[1], moe_b_group[1], moe_w_router[1], moe_b_router[1])
    x3, f, route, counts = _post_mix(att1, conv1, x2, x2, mods[1], o_w_out[0].astype(BF16), o_b_out[0][None, :],
                                     ln_g[1, 0][None, :], ln_b[1, 0][None, :], w_r, b_r, l_lat, l_lat)
    yb, dest = _moe(f, route, counts, moe_w1[1], moe_w3[1], moe_w2[1])
    return _post_ffn(x3, route, dest, yb, mods[1], ln_g[1, 1][None, :], ln_b[1, 1][None, :], l_lat)
```

```python
import functools

import numpy as np
import jax
import jax.numpy as jnp
from jax import lax
from jax.experimental import pallas as pl
from jax.experimental.pallas import tpu as pltpu

F32 = jnp.float32
BF16 = jnp.bfloat16

D_MODEL = 1024
DEPTH = 2
GRID_W = 64
ROPE_THETA = 10000.0
LN_EPS = 1e-6
NEG_INF = -1e30
DN_ALPHA = (2 * DEPTH) ** 0.25
LOG2E = 1.4426950408889634

LRU_WIDTH = 512
LRU_BLOCKS = 8
LRU_BLOCK_DIM = LRU_WIDTH // LRU_BLOCKS
LRU_CONV_W = 4
LRU_C = 8.0
SWA_HEADS = 8
SWA_KV_HEADS = 2
SWA_HEAD_DIM = 64
WINDOW = 128
MLA_HEADS = 8
MLA_Q_RANK = 256
MLA_KV_RANK = 128
MLA_NOPE = 64
MLA_ROPE = 32
MLA_V = 64
CONF_CH = 512
CONF_K = 31
N_GROUPS = 4
EXPERTS_PER_GROUP = 8
N_EXPERTS = N_GROUPS * EXPERTS_PER_GROUP
TOP_K = 2
D_EXPERT = 512

LANES = 128
SUBLANES = 8
ROW_TILE = 256
SCAN_SEG = ROW_TILE // SUBLANES
SCAN_PITCH = SCAN_SEG + 4
CONF_HALO = 16
CONF_SUB = 32
MOE_TILE = 256
DISPATCH_TILES = 4
FLASH_SUB = 256
FLASH_TQ = 1024
VMEM_LIMIT = 48 * 1024 * 1024


def _cparams(n_axes, vmem=VMEM_LIMIT):
    return pltpu.CompilerParams(dimension_semantics=("arbitrary",) * n_axes, vmem_limit_bytes=vmem)


def _dot(a, b):
    return jnp.dot(a, b, preferred_element_type=F32)


def _dot_nt(a, b):
    return lax.dot_general(a, b, (((1,), (1,)), ((), ())), preferred_element_type=F32)


def _split_bf16(a):
    hi = a.astype(BF16)
    lo = (a - hi.astype(F32)).astype(BF16)
    return hi, lo


def _silu(x):
    return x * jax.nn.sigmoid(x)


def _gelu_tanh(x):
    return 0.5 * x * (1.0 + jnp.tanh(np.sqrt(2.0 / np.pi).astype(np.float32) * (x + 0.044715 * (x * x * x))))


def _layer_norm(x, g, b):
    mu = jnp.mean(x, axis=-1, keepdims=True)
    xc = x - mu
    var = jnp.mean(xc * xc, axis=-1, keepdims=True)
    return xc * lax.rsqrt(var + LN_EPS) * g + b


def _select_mod(mb_ref, mc_ref, idx, is_ctx):
    return jnp.where(is_ctx, mc_ref[0, idx:idx + 1, :], mb_ref[0, idx:idx + 1, :])


def _mod_kernel(c_ref, w_ref, b_ref, o_ref):
    c = c_ref[...]
    s_hi, s_lo = _split_bf16(_silu(c))
    w_hi, w_lo = _split_bf16(w_ref[0])
    o_ref[0] = _dot(s_hi, w_hi) + _dot(s_lo, w_hi) + _dot(s_hi, w_lo) + b_ref[0]


def _modulation(cond8, w_mod, b_mod):
    depth, d, n = w_mod.shape
    tn = 1536
    return pl.pallas_call(
        _mod_kernel,
        grid=(depth, n // tn),
        in_specs=[pl.BlockSpec((SUBLANES, d), lambda l, j: (0, 0)),
                  pl.BlockSpec((1, d, tn), lambda l, j: (l, 0, j)),
                  pl.BlockSpec((1, 1, tn), lambda l, j: (l, 0, j))],
        out_specs=pl.BlockSpec((1, SUBLANES, tn), lambda l, j: (l, 0, j)),
        out_shape=jax.ShapeDtypeStruct((depth, SUBLANES, n), F32),
        compiler_params=_cparams(2),
        name="modulation",
    )(cond8, w_mod, b_mod.reshape(depth, 1, n))


def _rope_pattern(width, row, col):
    d = width // 2
    half = d // 2
    inv = ROPE_THETA ** (-jnp.arange(half, dtype=F32) / half)
    ang_r = row.astype(F32)[:, None] * inv[None, :]
    ang_c = col.astype(F32)[:, None] * inv[None, :]
    cos = jnp.concatenate([jnp.cos(ang_r)] * 2 + [jnp.cos(ang_c)] * 2, axis=1)
    sin = jnp.concatenate([-jnp.sin(ang_r), jnp.sin(ang_r), -jnp.sin(ang_c), jnp.sin(ang_c)], axis=1)
    partner = np.concatenate([np.arange(half) + half, np.arange(half), d + np.arange(half) + half, d + np.arange(half)])
    return cos, sin, partner


def _in_even_kernel(x_ref, c_ref, mb_ref, mc_ref, w_ref, b_ref, cos_ref, sin_ref,
                    g_ref, u_ref, q_ref, k_ref, v_ref, *, l_lat):
    i = pl.program_id(1)
    tm = x_ref.shape[1]
    rows = i * tm + lax.broadcasted_iota(jnp.int32, (tm, 1), 0)
    is_ctx = rows >= l_lat
    shift = _select_mod(mb_ref, mc_ref, 0, is_ctx)
    scale = _select_mod(mb_ref, mc_ref, 1, is_ctx)
    x_in = jnp.where(i * tm >= l_lat, c_ref[0], x_ref[0])
    h = (x_in * (1.0 + scale) + shift).astype(BF16)

    def proj(lo, hi):
        return _dot(h, w_ref[:, lo:hi]) + b_ref[:, lo:hi]

    w = LRU_WIDTH
    g_ref[0] = proj(0, w)
    u_ref[0] = proj(w, 2 * w)
    cos = cos_ref[...]
    sin = sin_ref[...]
    cos4 = jnp.concatenate([cos] * 4, axis=1)
    sin4 = jnp.concatenate([sin] * 4, axis=1)
    o = 2 * w
    q = (proj(o, o + 512) * cos4 + proj(o + 512, o + 1024) * sin4) * (SWA_HEAD_DIM ** -0.5 * LOG2E)
    q_ref[0] = q.astype(BF16)
    o += 1024
    cos2 = jnp.concatenate([cos] * 2, axis=1)
    sin2 = jnp.concatenate([sin] * 2, axis=1)
    k = proj(o, o + 256) * cos2 + proj(o + 256, o + 512) * sin2
    k_ref[0] = k.astype(BF16)
    o += 512
    v_ref[0] = proj(o, o + 256).astype(BF16)


def _in_even(x, ctx, mods, w, b, cos, sin):
    bsz, l_lat, d = x.shape
    s_all = l_lat + ctx.shape[1]
    tm = ROW_TILE
    n_lat = l_lat // tm
    nw = w.shape[1]
    row = lambda n: pl.BlockSpec((1, tm, n), lambda bb, i: (bb, i, 0))
    return pl.pallas_call(
        functools.partial(_in_even_kernel, l_lat=l_lat),
        grid=(bsz, s_all // tm),
        in_specs=[pl.BlockSpec((1, tm, d), lambda bb, i: (bb, jnp.minimum(i, n_lat - 1), 0)),
                  pl.BlockSpec((1, tm, d), lambda bb, i: (bb, 0, 0)),
                  pl.BlockSpec((1, 6, d), lambda bb, i: (bb, 0, 0)),
                  pl.BlockSpec((1, 6, d), lambda bb, i: (4, 0, 0)),
                  pl.BlockSpec((d, nw), lambda bb, i: (0, 0)),
                  pl.BlockSpec((1, nw), lambda bb, i: (0, 0)),
                  pl.BlockSpec((tm, LANES), lambda bb, i: (i, 0)),
                  pl.BlockSpec((tm, LANES), lambda bb, i: (i, 0))],
        out_specs=[row(512), row(512), row(512), row(256), row(256)],
        out_shape=[jax.ShapeDtypeStruct((bsz, s_all, 512), F32),
                   jax.ShapeDtypeStruct((bsz, s_all, 512), F32),
                   jax.ShapeDtypeStruct((bsz, s_all, 512), BF16),
                   jax.ShapeDtypeStruct((bsz, s_all, 256), BF16),
                   jax.ShapeDtypeStruct((bsz, s_all, 256), BF16)],
        compiler_params=_cparams(2),
        name="in_even",
    )(x, ctx, mods, mods, w, b, cos, sin)


def _softplus(x):
    return jnp.maximum(x, 0.0) + jnp.log1p(jnp.exp(-jnp.abs(x)))


def _lru_chunk_index(s, n_lat, reverse):
    if reverse:
        return jnp.where(s == 0, n_lat, n_lat - s)
    return jnp.where(s == 0, n_lat, s - 1)


def _lru_kernel(*refs, reverse, n_lat):
    if reverse:
        (u_ref, up_ref, un_ref, cw_ref, cb_ref, w_ref, bias_ref, lam_ref, g_ref, hf_ref,
         o_ref, ext, a_s, b_s, p_s, h_s, carry) = refs
    else:
        (u_ref, up_ref, un_ref, cw_ref, cb_ref, w_ref, bias_ref, lam_ref,
         o_ref, ext, a_s, b_s, p_s, h_s, carry) = refs
    s = pl.program_id(1)
    j = _lru_chunk_index(s, n_lat, reverse)
    t = ROW_TILE
    width = LRU_WIDTH

    @pl.when(s == 0)
    def _():
        carry[...] = jnp.zeros_like(carry)

    prev_ok = jnp.logical_and(j != 0, j != n_lat)
    next_ok = jnp.logical_and(j != n_lat - 1, j != n_lat)
    ext[0:SUBLANES, :] = jnp.where(prev_ok, up_ref[0], 0.0)
    ext[SUBLANES:SUBLANES + t, :] = u_ref[0]
    ext[SUBLANES + t:, :] = jnp.where(next_ok, un_ref[0], 0.0)
    uc = cb_ref[...]
    for kk in range(LRU_CONV_W):
        uc = uc + cw_ref[kk:kk + 1, :] * ext[SUBLANES - 1 + kk:SUBLANES - 1 + kk + t, :]

    z = _dot(uc.astype(BF16), w_ref[...]) + bias_ref[...]
    r = jax.nn.sigmoid(z[:, :width])
    ig = jax.nn.sigmoid(z[:, width:])
    log_a = (-LRU_C) * r * _softplus(-lam_ref[...])
    a = jnp.exp(log_a)
    b = jnp.sqrt(-jnp.tanh(log_a) * (a * a + 1.0)) * (ig * uc)

    steps = range(SCAN_SEG - 1, -1, -1) if reverse else range(SCAN_SEG)
    segs = range(SUBLANES - 1, -1, -1) if reverse else range(SUBLANES)
    blocks = []
    for cb in range(width // LANES):
        cols = slice(cb * LANES, (cb + 1) * LANES)
        for sg in range(SUBLANES):
            a_s[cb, sg * SCAN_PITCH:sg * SCAN_PITCH + SCAN_SEG, :] = a[sg * SCAN_SEG:(sg + 1) * SCAN_SEG, cols]
            b_s[cb, sg * SCAN_PITCH:sg * SCAN_PITCH + SCAN_SEG, :] = b[sg * SCAN_SEG:(sg + 1) * SCAN_SEG, cols]
        h = jnp.zeros((SUBLANES, LANES), F32)
        p = jnp.ones((SUBLANES, LANES), F32)
        for i in steps:
            sl = pl.ds(i, SUBLANES, stride=SCAN_PITCH)
            ai = a_s[cb, sl, :]
            h = ai * h + b_s[cb, sl, :]
            p = p * ai
            h_s[cb, sl, :] = h
            p_s[cb, sl, :] = p
        c = carry[:, cols]
        seg_in = [None] * SUBLANES
        for sg in segs:
            seg_in[sg] = c
            c = p[sg:sg + 1, :] * c + h[sg:sg + 1, :]
        carry[:, cols] = c
        cin = jnp.concatenate(seg_in, axis=0)
        for i in range(SCAN_SEG):
            sl = pl.ds(i, SUBLANES, stride=SCAN_PITCH)
            h_s[cb, sl, :] = h_s[cb, sl, :] + p_s[cb, sl, :] * cin
        blocks.append(jnp.concatenate([h_s[cb, sg * SCAN_PITCH:sg * SCAN_PITCH + SCAN_SEG, :]
                                       for sg in range(SUBLANES)], axis=0))
    hs = jnp.concatenate(blocks, axis=1)
    if reverse:
        o_ref[0] = (_gelu_tanh(g_ref[0]) * (hf_ref[0] + hs)).astype(o_ref.dtype)
    else:
        o_ref[0] = hs


def _lru(u, g, hf, conv_w, conv_b, w, bias, lam, reverse, l_lat):
    bsz, s_all, width = u.shape
    t = ROW_TILE
    n_chunks = s_all // t
    n_lat = l_lat // t
    hb = t // SUBLANES
    n_hb = s_all // SUBLANES
    cidx = functools.partial(_lru_chunk_index, n_lat=n_lat, reverse=reverse)
    chunk = pl.BlockSpec((1, t, width), lambda bb, s: (bb, cidx(s), 0))
    const = lambda shape: pl.BlockSpec(shape, lambda bb, s: (0,) * len(shape))
    in_specs = [chunk,
                pl.BlockSpec((1, SUBLANES, width), lambda bb, s: (bb, jnp.maximum(cidx(s) * hb - 1, 0), 0)),
                pl.BlockSpec((1, SUBLANES, width), lambda bb, s: (bb, jnp.minimum(cidx(s) * hb + hb, n_hb - 1), 0)),
                const((LRU_CONV_W, width)), const((1, width)), const((width, 2 * width)),
                const((1, 2 * width)), const((1, width))]
    args = [u, u, u, conv_w, conv_b, w, bias, lam]
    if reverse:
        in_specs += [chunk, chunk]
        args += [g, hf]
    return pl.pallas_call(
        functools.partial(_lru_kernel, reverse=reverse, n_lat=n_lat),
        grid=(bsz, n_chunks),
        in_specs=in_specs,
        out_specs=chunk,
        out_shape=jax.ShapeDtypeStruct((bsz, s_all, width), BF16 if reverse else F32),
        scratch_shapes=[pltpu.VMEM((t + 2 * SUBLANES, width), F32)]
                       + [pltpu.VMEM((width // LANES, SUBLANES * SCAN_PITCH, LANES), F32)] * 4
                       + [pltpu.VMEM((1, width), F32)],
        compiler_params=_cparams(2),
        name="lru_bwd" if reverse else "lru_fwd",
    )(*args)


def _swa_kernel(sink_ref, q_ref, k_ref, v_ref, o_ref, *, l_lat, n_ctx):
    i = pl.program_id(1)
    tq = q_ref.shape[1]
    wk = tq + 2 * WINDOW
    is_lat = i < l_lat // tq
    ws = pl.multiple_of(jnp.clip(i * tq - WINDOW, 0, l_lat - wk), WINDOW)
    qpos = i * tq + lax.broadcasted_iota(jnp.int32, (tq, 1), 0)
    kpos = ws + lax.broadcasted_iota(jnp.int32, (1, wk), 1)
    band = jnp.logical_and(jnp.abs(qpos - kpos) <= WINDOW, is_lat)
    valid = jnp.concatenate([band, jnp.ones((tq, n_ctx), jnp.bool_)], axis=1)
    lo = lax.broadcasted_iota(jnp.int32, (1, LANES), 1) < SWA_HEAD_DIM
    heads_per_kv = SWA_HEADS // SWA_KV_HEADS
    keys, vals = [], []
    for kh in range(SWA_KV_HEADS):
        cols = slice(LANES * kh, LANES * (kh + 1))
        keys.append(jnp.concatenate([k_ref[0, pl.ds(ws, wk), cols], k_ref[0, l_lat:l_lat + n_ctx, cols]], axis=0))
        vals.append(jnp.concatenate([v_ref[0, pl.ds(ws, wk), cols], v_ref[0, l_lat:l_lat + n_ctx, cols]], axis=0))
    for kh in range(SWA_KV_HEADS):
        heads = range(kh * heads_per_kv, (kh + 1) * heads_per_kv)
        scores = []
        for h in heads:
            qb = q_ref[0, :, LANES * (h // 2):LANES * (h // 2 + 1)]
            qm = jnp.where(lo if h % 2 == 0 else jnp.logical_not(lo), qb, jnp.zeros_like(qb))
            scores.append(_dot_nt(qm, keys[kh]))
        probs = []
        for h, sc in zip(heads, scores):
            s = jnp.where(valid, sc, NEG_INF)
            sink = sink_ref[h] * LOG2E
            m = jnp.maximum(jnp.max(s, axis=-1, keepdims=True), sink)
            p = jnp.exp2(s - m)
            probs.append((p.astype(BF16), jnp.sum(p, axis=-1, keepdims=True) + jnp.exp2(sink - m)))
        outs = [_dot(p, vals[kh]) / den for p, den in probs]
        for jq in range(heads_per_kv // 2):
            blk = kh * (heads_per_kv // 2) + jq
            o_ref[0, :, LANES * blk:LANES * (blk + 1)] = jnp.where(lo, outs[2 * jq], outs[2 * jq + 1]).astype(o_ref.dtype)


def _swa(q, k2, v2, sink, l_lat):
    bsz, s_all, _ = q.shape
    tq = ROW_TILE
    n_ctx = s_all - l_lat
    return pl.pallas_call(
        functools.partial(_swa_kernel, l_lat=l_lat, n_ctx=n_ctx),
        grid_spec=pltpu.PrefetchScalarGridSpec(
            num_scalar_prefetch=1,
            grid=(bsz, s_all // tq),
            in_specs=[pl.BlockSpec((1, tq, 512), lambda bb, i, s: (bb, i, 0)),
                      pl.BlockSpec((1, s_all, 256), lambda bb, i, s: (bb, 0, 0)),
                      pl.BlockSpec((1, s_all, 256), lambda bb, i, s: (bb, 0, 0))],
            out_specs=pl.BlockSpec((1, tq, 512), lambda bb, i, s: (bb, i, 0))),
        out_shape=jax.ShapeDtypeStruct((bsz, s_all, 512), BF16),
        compiler_params=_cparams(2),
        name="swa",
    )(sink, q, k2, v2)


def _route(logits):
    lane = lax.broadcasted_iota(jnp.int32, logits.shape, 1)
    lane_f = lane.astype(F32)

    def first_argmax(vals, mask):
        v = jnp.where(mask, vals, NEG_INF)
        m = jnp.max(v, axis=-1, keepdims=True)
        idx = jnp.min(jnp.where(jnp.logical_and(mask, v == m), lane_f, float(LANES)), axis=-1, keepdims=True)
        return m, idx.astype(jnp.int32)

    gmask = lane < N_GROUPS
    gm, gidx = first_argmax(logits, gmask)
    g_prob = 1.0 / jnp.sum(jnp.where(gmask, jnp.exp(logits - gm), 0.0), axis=-1, keepdims=True)
    base = N_GROUPS + gidx * EXPERTS_PER_GROUP
    emask = jnp.logical_and(lane >= base, lane < base + EXPERTS_PER_GROUP)
    m0, i0 = first_argmax(logits, emask)
    m1, i1 = first_argmax(logits, jnp.logical_and(emask, lane != i0))
    w1 = jnp.exp(m1 - m0)
    inv = g_prob / (1.0 + w1)
    return i0 - N_GROUPS, i1 - N_GROUPS, inv, w1 * inv


TOKEN_SUB = D_MODEL // LANES


def _store_token_rows(ref, val):
    n = val.shape[0]
    for j in range(TOKEN_SUB):
        ref[pl.ds(j, n, stride=TOKEN_SUB), :] = val[:, j * LANES:(j + 1) * LANES]


def _load_token_rows(ref, n):
    return jnp.concatenate([ref[pl.ds(j, n, stride=TOKEN_SUB), :] for j in range(TOKEN_SUB)], axis=1)


def _post_mix_kernel(a_ref, b_ref, x_ref, c_ref, mb_ref, mc_ref, w_ref, bo_ref, lng_ref, lnb_ref, wr_ref, br_ref,
                     x1_ref, f_ref, r_ref, cnt_ref, counts, *, l_lat):
    i = pl.program_id(1)
    tm = x_ref.shape[1]
    rows = i * tm + lax.broadcasted_iota(jnp.int32, (tm, 1), 0)
    is_ctx = rows >= l_lat
    half = a_ref.shape[2]
    y = _dot(a_ref[0], w_ref[0:half, :]) + _dot(b_ref[0], w_ref[half:, :]) + bo_ref[...]
    gate = _select_mod(mb_ref, mc_ref, 2, is_ctx)
    x_in = jnp.where(i * tm >= l_lat, c_ref[0], x_ref[0])
    x1 = _layer_norm(DN_ALPHA * x_in + gate * y, lng_ref[...], lnb_ref[...])
    x1_ref[0] = x1
    f = x1 * (1.0 + _select_mod(mb_ref, mc_ref, 4, is_ctx)) + _select_mod(mb_ref, mc_ref, 3, is_ctx)
    _store_token_rows(f_ref.at[0], f)
    f_hi, f_lo = _split_bf16(f)
    two = _dot(f_hi, wr_ref[...])
    logits = two[:, :LANES] + two[:, LANES:] + _dot(f_lo, wr_ref[:, :LANES]) + br_ref[...]
    e0, e1, g0, g1 = _route(logits)

    @pl.when(jnp.logical_and(pl.program_id(0) == 0, i == 0))
    def _():
        counts[...] = jnp.zeros_like(counts)

    lane = lax.broadcasted_iota(jnp.int32, (tm, LANES), 1)
    hot0 = lane == e0
    hot1 = lane == e1
    both = jnp.where(hot0, 1.0, 0.0) + jnp.where(hot1, 1.0, 0.0)
    tri = (lax.broadcasted_iota(jnp.int32, (tm, tm), 1) < lax.broadcasted_iota(jnp.int32, (tm, tm), 0))
    before = _dot(jnp.where(tri, 1.0, 0.0).astype(BF16), both.astype(BF16)) + counts[...]
    r0 = jnp.sum(jnp.where(hot0, before, 0.0), axis=-1, keepdims=True)
    r1 = jnp.sum(jnp.where(hot1, before, 0.0), axis=-1, keepdims=True)
    counts[...] = counts[...] + jnp.sum(both, axis=0, keepdims=True)
    cnt_ref[...] = counts[...]
    r_ref[0] = jnp.where(lane == 0, e0.astype(F32), jnp.where(lane == 1, e1.astype(F32), jnp.where(
        lane == 2, g0, jnp.where(lane == 3, g1, jnp.where(lane == 4, r0, jnp.where(lane == 5, r1, 0.0))))))


def _post_mix(a, b, x, ctx, mods, w_out, b_out, ln_g, ln_b, w_r, b_r, n_rows, l_lat):
    bsz = x.shape[0]
    d = x.shape[2]
    tm = ROW_TILE
    half = a.shape[2]
    n_lat = l_lat // tm
    row = lambda n: pl.BlockSpec((1, tm, n), lambda bb, i: (bb, i, 0))
    const = lambda shape: pl.BlockSpec(shape, lambda bb, i: (0,) * len(shape))
    return pl.pallas_call(
        functools.partial(_post_mix_kernel, l_lat=l_lat),
        grid=(bsz, n_rows // tm),
        in_specs=[row(half), row(half),
                  pl.BlockSpec((1, tm, d), lambda bb, i: (bb, jnp.minimum(i, n_lat - 1), 0)),
                  pl.BlockSpec((1, tm, d), lambda bb, i: (bb, 0, 0)),
                  pl.BlockSpec((1, 6, d), lambda bb, i: (bb, 0, 0)),
                  pl.BlockSpec((1, 6, d), lambda bb, i: (4, 0, 0)),
                  const((2 * half, d)), const((1, d)), const((1, d)), const((1, d)),
                  const((d, 2 * LANES)), const((1, LANES))],
        out_specs=[row(d), pl.BlockSpec((1, tm * TOKEN_SUB, LANES), lambda bb, i: (bb, i, 0)), row(LANES),
                   const((1, LANES))],
        out_shape=[jax.ShapeDtypeStruct((bsz, n_rows, d), F32),
                   jax.ShapeDtypeStruct((bsz, n_rows * TOKEN_SUB, LANES), F32),
                   jax.ShapeDtypeStruct((bsz, n_rows, LANES), F32),
                   jax.ShapeDtypeStruct((1, LANES), F32)],
        scratch_shapes=[pltpu.VMEM((1, LANES), F32)],
        compiler_params=_cparams(2),
        name="post_mix",
    )(a, b, x, ctx, mods, mods, w_out, b_out, ln_g, ln_b, w_r, b_r)


def _token_rows(ref, tok, n=1):
    start = tok * TOKEN_SUB
    if not isinstance(start, int):
        start = pl.multiple_of(start, TOKEN_SUB)
    return ref.at[pl.ds(start, n * TOKEN_SUB), :]


def _dispatch_kernel(pad_ref, nb_ref, dest_ref, f_ref, xb_ref, zeros, sem_z, sem):
    i = pl.program_id(0)
    tiles = dest_ref.shape[0]
    tm = f_ref.shape[0] // TOKEN_SUB // tiles
    tb = zeros.shape[0] // TOKEN_SUB
    n_blocks = xb_ref.shape[0] // zeros.shape[0]

    @pl.when(i == 0)
    def _():
        zeros[...] = jnp.zeros_like(zeros)
        fills = [pltpu.make_async_copy(zeros, _token_rows(xb_ref, pad_ref[e], tb), sem_z) for e in range(N_EXPERTS)]
        for cp in fills:
            cp.start()
        for cp in fills:
            cp.wait()

        def tail(b):
            return pltpu.make_async_copy(zeros, _token_rows(xb_ref, b * tb, tb), sem_z)

        def start(b, carry):
            tail(b).start()
            return carry

        def wait(b, carry):
            tail(b).wait()
            return carry

        lax.fori_loop(nb_ref[0], n_blocks, start, 0)
        lax.fori_loop(nb_ref[0], n_blocks, wait, 0)

    copies = [pltpu.make_async_copy(_token_rows(f_ref, t * tm + r),
                                    _token_rows(xb_ref, dest_ref[t, 0, kk * tm + r]), sem)
              for t in range(tiles) for r in range(tm) for kk in range(TOP_K)]
    for n, cp in enumerate(copies):
        cp.start(priority=n % 2)
    for cp in copies:
        cp.wait()


def _dispatch(pad_start, n_used, dest_tiles, tokens, n_slots):
    n_tiles = dest_tiles.shape[0]
    tm = ROW_TILE
    tb = MOE_TILE
    tiles = next(t for t in (DISPATCH_TILES, 2, 1) if n_tiles % t == 0)
    return pl.pallas_call(
        _dispatch_kernel,
        grid_spec=pltpu.PrefetchScalarGridSpec(
            num_scalar_prefetch=2,
            grid=(n_tiles // tiles,),
            in_specs=[pl.BlockSpec((tiles, 1, TOP_K * tm), lambda i, pad, nb: (i, 0, 0), memory_space=pltpu.SMEM),
                      pl.BlockSpec((tiles * tm * TOKEN_SUB, LANES), lambda i, pad, nb: (i, 0))],
            out_specs=pl.BlockSpec(memory_space=pl.ANY),
            scratch_shapes=[pltpu.VMEM((tb * TOKEN_SUB, LANES), F32), pltpu.SemaphoreType.DMA,
                            pltpu.SemaphoreType.DMA]),
        out_shape=jax.ShapeDtypeStruct(((n_slots + tb) * TOKEN_SUB, LANES), F32),
        compiler_params=pltpu.CompilerParams(dimension_semantics=("arbitrary",), vmem_limit_bytes=VMEM_LIMIT,
                                             disable_bounds_checks=True),
        name="moe_dispatch",
    )(pad_start, n_used, dest_tiles, tokens)


def _expert_kernel(be_ref, nb_ref, x_ref, w1_ref, w3_ref, w2_ref, o_ref, w1b, w3b, w2b):
    i = pl.program_id(0)
    prev = be_ref[jnp.maximum(i - 1, 0)]

    @pl.when(jnp.logical_or(i == 0, be_ref[i] != prev))
    def _():
        w1b[...] = w1_ref[0, 0].astype(BF16)
        w3b[...] = w3_ref[0, 0].astype(BF16)
        w2b[...] = w2_ref[0, 0].astype(BF16)

    @pl.when(i < nb_ref[0])
    def _():
        xb = _load_token_rows(x_ref, x_ref.shape[0] // TOKEN_SUB).astype(BF16)
        h1 = _dot(xb, w1b[...])
        h3 = _dot(xb, w3b[...])
        act = (_silu(h1) * h3).astype(BF16)
        _store_token_rows(o_ref, _dot(act, w2b[...]))

    @pl.when(i >= nb_ref[0])
    def _():
        o_ref[...] = jnp.zeros_like(o_ref)


def _experts(block_expert, n_used, xb, w1, w3, w2, layer):
    tb = MOE_TILE
    n_blocks = block_expert.shape[0]
    _, _, d, de = w1.shape
    last = lambda i, nb: jnp.minimum(i, jnp.maximum(nb[0] - 1, 0))
    return pl.pallas_call(
        _expert_kernel,
        grid_spec=pltpu.PrefetchScalarGridSpec(
            num_scalar_prefetch=2,
            grid=(n_blocks,),
            in_specs=[pl.BlockSpec((tb * TOKEN_SUB, LANES), lambda i, be, nb: (last(i, nb), 0)),
                      pl.BlockSpec((1, 1, d, de), lambda i, be, nb: (layer, be[i], 0, 0)),
                      pl.BlockSpec((1, 1, d, de), lambda i, be, nb: (layer, be[i], 0, 0)),
                      pl.BlockSpec((1, 1, de, d), lambda i, be, nb: (layer, be[i], 0, 0))],
            out_specs=pl.BlockSpec((tb * TOKEN_SUB, LANES), lambda i, be, nb: (i, 0)),
            scratch_shapes=[pltpu.VMEM((d, de), BF16), pltpu.VMEM((d, de), BF16), pltpu.VMEM((de, d), BF16)]),
        out_shape=jax.ShapeDtypeStruct((n_blocks * tb * TOKEN_SUB, LANES), F32),
        compiler_params=_cparams(1),
        name="experts",
    )(block_expert, n_used, xb, w1, w3, w2)


def _post_ffn_kernel(dcur_ref, dnxt_ref, x_ref, r_ref, yb_ref, mb_ref, mc_ref, lng_ref, lnb_ref, o_ref, ybuf, sems,
                     *, l_lat):
    bb, i = pl.program_id(0), pl.program_id(1)
    nt = pl.num_programs(1)
    step = bb * nt + i
    n_steps = pl.num_programs(0) * nt
    tm = x_ref.shape[1]

    def gathers(dref, slot):
        return [pltpu.make_async_copy(_token_rows(yb_ref, dref[0, 0, kk * tm + r]),
                                      _token_rows(ybuf.at[slot, kk], r), sems.at[slot])
                for r in range(tm) for kk in range(TOP_K)]

    def issue(dref, slot):
        for n, cp in enumerate(gathers(dref, slot)):
            cp.start(priority=n % 2)

    cur = step % 2

    @pl.when(step == 0)
    def _():
        issue(dcur_ref, 0)

    @pl.when(step + 1 < n_steps)
    def _():
        issue(dnxt_ref, 1 - cur)

    for cp in gathers(dcur_ref, cur):
        cp.wait()

    rows = i * tm + lax.broadcasted_iota(jnp.int32, (tm, 1), 0)
    is_ctx = rows >= l_lat
    gate = _select_mod(mb_ref, mc_ref, 5, is_ctx)
    r = r_ref[0]
    ffn = r[:, 2:3] * _load_token_rows(ybuf.at[cur, 0], tm) + r[:, 3:4] * _load_token_rows(ybuf.at[cur, 1], tm)
    o_ref[0] = _layer_norm(DN_ALPHA * x_ref[0] + gate * ffn, lng_ref[...], lnb_ref[...])


def _post_ffn(x1, route, dest_tiles, yb, mods, ln_g, ln_b, l_lat):
    bsz, n_rows, d = x1.shape
    tm = ROW_TILE
    nt = n_rows // tm
    n_steps = bsz * nt
    row = lambda n: pl.BlockSpec((1, tm, n), lambda bb, i: (bb, i, 0))
    const = lambda shape: pl.BlockSpec(shape, lambda bb, i: (0,) * len(shape))
    dspec = lambda ahead: pl.BlockSpec((1, 1, TOP_K * tm),
                                       lambda bb, i: (jnp.minimum(bb * nt + i + ahead, n_steps - 1), 0, 0),
                                       memory_space=pltpu.SMEM)
    return pl.pallas_call(
        functools.partial(_post_ffn_kernel, l_lat=l_lat),
        grid=(bsz, nt),
        in_specs=[dspec(0), dspec(1), row(d), row(LANES), pl.BlockSpec(memory_space=pl.ANY),
                  pl.BlockSpec((1, 6, d), lambda bb, i: (bb, 0, 0)),
                  pl.BlockSpec((1, 6, d), lambda bb, i: (4, 0, 0)),
                  const((1, d)), const((1, d))],
        out_specs=row(d),
        out_shape=jax.ShapeDtypeStruct((bsz, n_rows, d), F32),
        scratch_shapes=[pltpu.VMEM((2, TOP_K, tm * TOKEN_SUB, LANES), F32), pltpu.SemaphoreType.DMA((2,))],
        compiler_params=pltpu.CompilerParams(dimension_semantics=("arbitrary", "arbitrary"),
                                             vmem_limit_bytes=VMEM_LIMIT, disable_bounds_checks=True),
        name="post_ffn",
    )(dest_tiles, dest_tiles, x1, route, yb, mods, mods, ln_g, ln_b)


def _moe(f, route, counts, w1, w3, w2, layer):
    bsz, n_rows, _ = route.shape
    n_tok = bsz * n_rows
    tb = MOE_TILE
    tm = ROW_TILE
    r = route.reshape(n_tok, LANES)
    expert = r[:, 0:TOP_K].astype(jnp.int32)
    rank = r[:, 2 * TOP_K:3 * TOP_K].astype(jnp.int32)
    cnt = counts[0, :N_EXPERTS].astype(jnp.int32)
    padded = (cnt + tb - 1) // tb * tb
    p_ends = jnp.cumsum(padded)
    p_starts = p_ends - padded
    ids = jnp.arange(N_EXPERTS, dtype=jnp.int32)
    dest = jnp.sum(jnp.where(expert[:, :, None] == ids[None, None, :], p_starts[None, None, :], 0), axis=-1) + rank
    dest_tiles = dest.reshape(n_tok // tm, tm, TOP_K).transpose(0, 2, 1).reshape(n_tok // tm, 1, TOP_K * tm)
    n_blocks = -(-(n_tok * TOP_K + N_EXPERTS * (tb - 1)) // tb)
    block_start = jnp.arange(n_blocks, dtype=jnp.int32) * tb
    block_expert = jnp.minimum(jnp.sum((p_ends[None, :] <= block_start[:, None]).astype(jnp.int32), axis=1),
                               N_EXPERTS - 1)
    n_used = (p_ends[-1] // tb).reshape(1)
    xb = _dispatch(p_starts + cnt, n_used, dest_tiles, f.reshape(n_tok * TOKEN_SUB, LANES), n_blocks * tb)
    return _experts(block_expert, n_used, xb, w1, w3, w2, layer), dest_tiles


def _rms(x, g):
    return x * lax.rsqrt(jnp.mean(x * x, axis=-1, keepdims=True) + LN_EPS) * g

def _in_odd_kernel(x_ref, mb_ref, mc_ref, w_ref, b_ref, qn_ref, kvn_ref, wuq_ref, wuk_ref, place_ref, wuvt_ref,
                   cq_ref, sq_ref, ck_ref, sk_ref, q_ref, k_ref, vt_ref, glu_ref, *, l_lat):
    i = pl.program_id(0)
    tm = x_ref.shape[1]
    rows = i * tm + lax.broadcasted_iota(jnp.int32, (tm, 1), 0)
    is_ctx = rows >= l_lat
    shift = _select_mod(mb_ref, mc_ref, 0, is_ctx)
    scale = _select_mod(mb_ref, mc_ref, 1, is_ctx)
    h = (x_ref[0] * (1.0 + scale) + shift).astype(BF16)

    def proj(lo, hi):
        return _dot(h, w_ref[:, lo:hi]) + b_ref[:, lo:hi]

    cqn = _rms(proj(0, MLA_Q_RANK), qn_ref[...]).astype(BF16)
    hw = MLA_HEADS * LANES
    cos8 = jnp.concatenate([cq_ref[...]] * MLA_HEADS, axis=1)
    sin8 = jnp.concatenate([sq_ref[...]] * MLA_HEADS, axis=1)
    q = _dot(cqn, wuq_ref[:, :hw]) * cos8 + _dot(cqn, wuq_ref[:, hw:]) * sin8
    q_ref[0] = (q * ((MLA_NOPE + MLA_ROPE) ** -0.5 * LOG2E)).astype(BF16)
    o = MLA_Q_RANK
    cn = _rms(proj(o, o + MLA_KV_RANK), kvn_ref[...]).astype(BF16)
    o += MLA_KV_RANK
    kpe = proj(o, o + LANES) * ck_ref[...] + proj(o + LANES, o + 2 * LANES) * sk_ref[...]
    k_ref[0] = (_dot(cn, wuk_ref[...]) + _dot(kpe.astype(BF16), place_ref[...])).astype(BF16)
    vt = _dot_nt(wuvt_ref[...], cn)
    vt_ref[0] = vt.reshape(MLA_HEADS, 1, MLA_V, tm).astype(BF16)
    o += 2 * LANES
    glu_ref[0] = proj(o, o + CONF_CH) * jax.nn.sigmoid(proj(o + CONF_CH, o + 2 * CONF_CH))


def _in_odd(x_all, mods, w, b, qn, kvn, wuq, wuk, place, wuvt, cq, sq, ck, sk, l_lat, tm):
    bsz, s_all, d = x_all.shape
    nw = w.shape[1]
    nk = s_all // tm
    hw = MLA_HEADS * LANES
    row = lambda n: pl.BlockSpec((1, tm, n), lambda i, bb: (bb, i, 0))
    const = lambda shape: pl.BlockSpec(shape, lambda i, bb: (0,) * len(shape))
    tab = pl.BlockSpec((tm, LANES), lambda i, bb: (i, 0))
    return pl.pallas_call(
        functools.partial(_in_odd_kernel, l_lat=l_lat),
        grid=(nk, bsz),
        in_specs=[row(d),
                  pl.BlockSpec((1, 6, d), lambda i, bb: (bb, 0, 0)),
                  pl.BlockSpec((1, 6, d), lambda i, bb: (4, 0, 0)),
                  const((d, nw)), const((1, nw)), const((1, MLA_Q_RANK)), const((1, MLA_KV_RANK)),
                  const((MLA_Q_RANK, 2 * hw)), const((MLA_KV_RANK, hw)), const((LANES, hw)),
                  const((MLA_HEADS * MLA_V, MLA_KV_RANK)), tab, tab, tab, tab],
        out_specs=[row(hw), row(hw),
                   pl.BlockSpec((1, MLA_HEADS, 1, MLA_V, tm), lambda i, bb: (bb, 0, i, 0, 0)),
                   row(CONF_CH)],
        out_shape=[jax.ShapeDtypeStruct((bsz, s_all, hw), BF16),
                   jax.ShapeDtypeStruct((bsz, s_all, hw), BF16),
                   jax.ShapeDtypeStruct((bsz, MLA_HEADS, nk, MLA_V, tm), BF16),
                   jax.ShapeDtypeStruct((bsz, s_all, CONF_CH), F32)],
        compiler_params=_cparams(2),
        name="in_odd",
    )(x_all, mods, mods, w, b, qn, kvn, wuq, wuk, place, wuvt, cq, sq, ck, sk)


def _flash_kernel(q_ref, k_ref, vt_ref, o_ref, s_scr, *, tk, nk):
    tq = q_ref.shape[1]
    chains = [(hh, j) for hh in range(2) for j in range(tq // FLASH_SUB)]

    def scores(c, slot):
        off = c * tk if isinstance(c, int) else pl.multiple_of(c * tk, tk)
        for ci, (hh, j) in enumerate(chains):
            q = q_ref[0, j * FLASH_SUB:(j + 1) * FLASH_SUB, LANES * hh:LANES * (hh + 1)]
            k = k_ref[0, pl.ds(off, tk), LANES * hh:LANES * (hh + 1)]
            s_scr[slot, ci] = _dot_nt(k, q)

    def update(c, slot, carry):
        probs = []
        for ci, (m, l, acc) in enumerate(carry):
            m_new = jnp.maximum(m, jnp.max(s_scr[slot, ci], axis=0, keepdims=True))
            alpha = jnp.exp2(m - m_new)
            p = jnp.exp2(s_scr[slot, ci] - m_new)
            probs.append((m_new, alpha, alpha * l + jnp.sum(p, axis=0, keepdims=True), p.astype(BF16)))
        out = []
        for (hh, j), (m_new, alpha, l, p), (_, _, acc) in zip(chains, probs, carry):
            out.append((m_new, l, alpha * acc + _dot(vt_ref[0, hh, c], p)))
        return tuple(out)

    def pair(i, carry):
        c = 2 * i
        scores(c + 1, 1)
        carry = update(c, 0, carry)
        scores(c + 2, 0)
        return update(c + 1, 1, carry)

    init = tuple((jnp.full((1, FLASH_SUB), NEG_INF, F32), jnp.zeros((1, FLASH_SUB), F32),
                  jnp.zeros((MLA_V, FLASH_SUB), F32)) for _ in chains)
    scores(0, 0)
    n_pairs = (nk - 1) // 2
    res = lax.fori_loop(0, n_pairs, pair, init)
    if nk - 2 * n_pairs == 2:
        scores(nk - 1, 1)
        res = update(nk - 2, 0, res)
        res = update(nk - 1, 1, res)
    else:
        res = update(nk - 1, 0, res)
    norm = {ch: acc / l for ch, (m, l, acc) in zip(chains, res)}
    for j in range(tq // FLASH_SUB):
        pair = jnp.concatenate([norm[(0, j)], norm[(1, j)]], axis=0)
        o_ref[0, j * FLASH_SUB:(j + 1) * FLASH_SUB, :] = pair.T.astype(o_ref.dtype)


def _flash(q, k, vt, l_lat, tk):
    bsz, s_all, _ = q.shape
    nk = s_all // tk
    tq = FLASH_TQ if l_lat % FLASH_TQ == 0 else FLASH_SUB
    return pl.pallas_call(
        functools.partial(_flash_kernel, tk=tk, nk=nk),
        grid=(bsz, MLA_HEADS // 2, l_lat // tq),
        in_specs=[pl.BlockSpec((1, tq, 2 * LANES), lambda bb, hp, i: (bb, i, hp)),
                  pl.BlockSpec((1, s_all, 2 * LANES), lambda bb, hp, i: (bb, 0, hp)),
                  pl.BlockSpec((1, 2, nk, MLA_V, tk), lambda bb, hp, i: (bb, hp, 0, 0, 0))],
        out_specs=pl.BlockSpec((1, tq, LANES), lambda bb, hp, i: (bb, i, hp)),
        out_shape=jax.ShapeDtypeStruct((bsz, l_lat, MLA_HEADS * MLA_V), BF16),
        scratch_shapes=[pltpu.VMEM((2, 2 * tq // FLASH_SUB, tk, FLASH_SUB), F32)],
        compiler_params=_cparams(3),
        name="mla_attention",
    )(q, k, vt)


def _conf_kernel(x_ref, xp_ref, xn_ref, w_ref, b_ref, g_ref, bb_ref, o_ref, ext, shifted):
    i = pl.program_id(1)
    n = pl.num_programs(1)
    t = x_ref.shape[1]
    hl = CONF_HALO
    ext[0:hl, :] = jnp.where(i > 0, xp_ref[0], 0.0)
    ext[hl:hl + t, :] = x_ref[0]
    ext[hl + t:, :] = jnp.where(i < n - 1, xn_ref[0], 0.0)
    span = shifted.shape[1]
    for s in range(SUBLANES):
        shifted[s] = ext[s:s + span, :]
    off = hl - CONF_K // 2
    for r in range(t // CONF_SUB):
        acc = jnp.zeros((CONF_SUB, CONF_CH), F32) + b_ref[...]
        for kk in range(CONF_K):
            st = off + kk
            base = st // SUBLANES * SUBLANES + r * CONF_SUB
            acc = acc + w_ref[kk:kk + 1, :] * shifted[st % SUBLANES, base:base + CONF_SUB, :]
        o_ref[0, r * CONF_SUB:(r + 1) * CONF_SUB, :] = _silu(_layer_norm(acc, g_ref[...], bb_ref[...])).astype(o_ref.dtype)


def _conformer(glu, w, b, g, beta, l_lat):
    bsz, _, ch = glu.shape
    t = ROW_TILE
    hl = CONF_HALO
    hb = t // hl
    n_hb = l_lat // hl
    const = lambda shape: pl.BlockSpec(shape, lambda bb, i: (0,) * len(shape))
    return pl.pallas_call(
        _conf_kernel,
        grid=(bsz, l_lat // t),
        in_specs=[pl.BlockSpec((1, t, ch), lambda bb, i: (bb, i, 0)),
                  pl.BlockSpec((1, hl, ch), lambda bb, i: (bb, jnp.maximum(i * hb - 1, 0), 0)),
                  pl.BlockSpec((1, hl, ch), lambda bb, i: (bb, jnp.minimum(i * hb + hb, n_hb - 1), 0)),
                  const((CONF_K, ch)), const((1, ch)), const((1, ch)), const((1, ch))],
        out_specs=pl.BlockSpec((1, t, ch), lambda bb, i: (bb, i, 0)),
        out_shape=jax.ShapeDtypeStruct((bsz, l_lat, ch), BF16),
        scratch_shapes=[pltpu.VMEM((t + 2 * hl, ch), F32),
                        pltpu.VMEM((SUBLANES, t + 2 * hl - SUBLANES, ch), F32)],
        compiler_params=_cparams(2),
        name="conformer_conv",
    )(glu, glu, glu, w, b, g, beta)


def _block_diag(w):
    nb, c, d = w.shape
    out = jnp.zeros((nb, c, nb, d), w.dtype)
    out = out.at[jnp.arange(nb), :, jnp.arange(nb), :].set(w)
    return out.reshape(nb * c, nb * d)


def _router_weights(w_group, b_group, w_router, b_router):
    d = w_group.shape[0]
    w = jnp.zeros((d, LANES), F32).at[:, :N_GROUPS].set(w_group).at[:, N_GROUPS:N_GROUPS + N_EXPERTS].set(w_router)
    hi, lo = _split_bf16(w)
    b = jnp.zeros((1, LANES), F32).at[0, :N_GROUPS].set(b_group).at[0, N_GROUPS:N_GROUPS + N_EXPERTS].set(b_router)
    return jnp.concatenate([hi, lo], axis=1), b


def kernel(x, c, ctx, c_ctx, w_mod, b_mod, ln_g, ln_b, e_w_in, e_b_in, e_conv_w, e_conv_b, e_lru_wa, e_lru_ba, e_lru_wx, e_lru_bx, e_lru_lambda, e_sink, e_w_out, e_b_out, o_w_in, o_b_in, o_q_norm, o_kv_norm, o_w_uq, o_w_uk, o_w_uv, o_dw_w, o_dw_b, o_cln_g, o_cln_b, o_w_out, o_b_out, moe_w_group, moe_b_group, moe_w_router, moe_b_router, moe_w1, moe_w3, moe_w2):
    bsz, l_lat, d = x.shape
    n_ctx = ctx.shape[1]
    s_all = l_lat + n_ctx
    assert d == D_MODEL and bsz <= 4 and l_lat % ROW_TILE == 0 and n_ctx == ROW_TILE and l_lat % GRID_W == 0
    assert w_mod.shape[0] == DEPTH == 2

    cond8 =jnp.zeros((SUBLANES, d), F32).at[:bsz].set(c).at[4].set(c_ctx)
    mods = _modulation(cond8, w_mod, b_mod).reshape(DEPTH, SUBLANES, 6, d)

    t_idx = jnp.arange(l_lat, dtype=jnp.int32)
    row_pos, col_pos = t_idx // GRID_W, t_idx % GRID_W

    cos64, sin64, partner64 = _rope_pattern(SWA_HEAD_DIM, row_pos, col_pos)
    cos_e = jnp.ones((s_all, LANES), F32).at[:l_lat].set(jnp.concatenate([cos64] * 2, axis=1))
    sin_e = jnp.zeros((s_all, LANES), F32).at[:l_lat].set(jnp.concatenate([sin64] * 2, axis=1))
    w_in, b_in = e_w_in[0], e_b_in[0]
    oq = 2 * LRU_WIDTH
    ok = oq + SWA_HEADS * SWA_HEAD_DIM
    ov = ok + SWA_KV_HEADS * SWA_HEAD_DIM
    q_cols = oq + np.arange(SWA_HEADS * SWA_HEAD_DIM)
    qp_cols = oq + (np.arange(SWA_HEADS)[:, None] * SWA_HEAD_DIM + partner64[None, :]).reshape(-1)
    dup = np.repeat(np.arange(SWA_KV_HEADS), 2)
    k_cols = ok + (dup[:, None] * SWA_HEAD_DIM + np.arange(SWA_HEAD_DIM)[None, :]).reshape(-1)
    kp_cols = ok + (dup[:, None] * SWA_HEAD_DIM + partner64[None, :]).reshape(-1)
    v_cols = ov + (dup[:, None] * SWA_HEAD_DIM + np.arange(SWA_HEAD_DIM)[None, :]).reshape(-1)
    cols = np.concatenate([np.arange(oq), q_cols, qp_cols, k_cols, kp_cols, v_cols])
    g0, u0, q0, k0, v0 = _in_even(x, ctx, mods[0], w_in[:, cols].astype(BF16), b_in[cols][None, :], cos_e, sin_e)

    def lru_weights(dd):
        w = jnp.concatenate([_block_diag(e_lru_wa[0, dd]), _block_diag(e_lru_wx[0, dd])], axis=1).astype(BF16)
        bias = jnp.concatenate([e_lru_ba[0, dd], e_lru_bx[0, dd]])[None, :]
        return w, bias, e_lru_lambda[0, dd][None, :]

    cw, cb = e_conv_w[0], e_conv_b[0][None, :]
    hf = _lru(u0, None, None, cw, cb, *lru_weights(0), reverse=False, l_lat=l_lat)
    rec = _lru(u0, g0, hf, cw, cb, *lru_weights(1), reverse=True, l_lat=l_lat)
    att = _swa(q0, k0, v0, e_sink[0], l_lat)
    w_r, b_r = _router_weights(moe_w_group[0], moe_b_group[0], moe_w_router[0], moe_b_router[0])
    x1, f, route, counts = _post_mix(rec, att, x, ctx, mods[0], e_w_out[0].astype(BF16), e_b_out[0][None, :],
                                     ln_g[0, 0][None, :], ln_b[0, 0][None, :], w_r, b_r, s_all, l_lat)
    yb, dest = _moe(f, route, counts, moe_w1, moe_w3, moe_w2, 0)
    x2 = _post_ffn(x1, route, dest, yb, mods[0], ln_g[0, 1][None, :], ln_b[0, 1][None, :], l_lat)

    tk = 768 if s_all % 768 == 0 else ROW_TILE
    cos32, sin32, partner32 = _rope_pattern(MLA_ROPE, row_pos, col_pos)
    w_in, b_in = o_w_in[0], o_b_in[0]
    o_pe = MLA_Q_RANK + MLA_KV_RANK
    o_cv = o_pe + MLA_ROPE

    def pad_cols(wm, bv, n):
        return (jnp.zeros((wm.shape[0], n), F32).at[:, :wm.shape[1]].set(wm),
                jnp.zeros((n,), F32).at[:bv.shape[0]].set(bv))

    pe_cols = o_pe + np.arange(MLA_ROPE)
    wa_, ba_ = pad_cols(w_in[:, pe_cols], b_in[pe_cols], LANES)
    wb_, bb_ = pad_cols(w_in[:, o_pe + partner32], b_in[o_pe + partner32], LANES)
    w_odd = jnp.concatenate([w_in[:, :o_pe], wa_, wb_, w_in[:, o_cv:]], axis=1).astype(BF16)
    b_odd = jnp.concatenate([b_in[:o_pe], ba_, bb_, b_in[o_cv:]])[None, :]
    ck = jnp.ones((s_all, LANES), F32).at[:l_lat, :MLA_ROPE].set(cos32)
    sk = jnp.zeros((s_all, LANES), F32).at[:l_lat, :MLA_ROPE].set(sin32)
    cq = jnp.ones((s_all, LANES), F32).at[:l_lat, MLA_NOPE:MLA_NOPE + MLA_ROPE].set(cos32)
    sq = jnp.zeros((s_all, LANES), F32).at[:l_lat, MLA_NOPE:MLA_NOPE + MLA_ROPE].set(sin32)
    hd = MLA_NOPE + MLA_ROPE
    wuq = o_w_uq[0].reshape(MLA_Q_RANK, MLA_HEADS, hd)
    wuq_main = jnp.zeros((MLA_Q_RANK, MLA_HEADS, LANES), F32).at[:, :, :hd].set(wuq)
    wuq_part = jnp.zeros((MLA_Q_RANK, MLA_HEADS, LANES), F32).at[:, :, MLA_NOPE:hd].set(wuq[:, :, MLA_NOPE + partner32])
    hw = MLA_HEADS * LANES
    wuq_ext = jnp.concatenate([wuq_main.reshape(MLA_Q_RANK, hw), wuq_part.reshape(MLA_Q_RANK, hw)], axis=1).astype(BF16)
    wuk = jnp.zeros((MLA_KV_RANK, MLA_HEADS, LANES), F32).at[:, :, :MLA_NOPE].set(
        o_w_uk[0].reshape(MLA_KV_RANK, MLA_HEADS, MLA_NOPE)).reshape(MLA_KV_RANK, hw).astype(BF16)
    place = np.zeros((LANES, MLA_HEADS, LANES), np.float32)
    for hh in range(MLA_HEADS):
        place[np.arange(MLA_ROPE), hh, MLA_NOPE + np.arange(MLA_ROPE)] = 1.0
    place = jnp.asarray(place.reshape(LANES, hw), BF16)
    wuvt = o_w_uv[0].T.astype(BF16)
    q1, k1, vt1, glu = _in_odd(x2, mods[1], w_odd, b_odd, o_q_norm[0][None, :], o_kv_norm[0][None, :], wuq_ext, wuk,
                               place, wuvt, cq, sq, ck, sk, l_lat, tk)
    att1 = _flash(q1, k1, vt1, l_lat, tk)
    conv1 = _conformer(glu, o_dw_w[0], o_dw_b[0][None, :], o_cln_g[0][None, :], o_cln_b[0][None, :], l_lat)
    w_r, b_r = _router_weights(moe_w_group[1], moe_b_group[1], moe_w_router[1], moe_b_router[1])
    x3, f, route, counts = _post_mix(att1, conv1, x2, x2, mods[1], o_w_out[0].astype(BF16), o_b_out[0][None, :],
                                     ln_g[1, 0][None, :], ln_b[1, 0][None, :], w_r, b_r, l_lat, l_lat)
    yb, dest = _moe(f, route, counts, moe_w1, moe_w3, moe_w2, 1)
    return _post_ffn(x3, route, dest, yb, mods[1], ln_g[1, 1][None, :], ln_b[1, 1][None, :], l_lat)
```

```python
import functools

import numpy as np
import jax
import jax.numpy as jnp
from jax import lax
from jax.experimental import pallas as pl
from jax.experimental.pallas import tpu as pltpu

F32 = jnp.float32
BF16 = jnp.bfloat16

D_MODEL = 1024
DEPTH = 2
GRID_W = 64
ROPE_THETA = 10000.0
LN_EPS = 1e-6
NEG_INF = -1e30
DN_ALPHA = (2 * DEPTH) ** 0.25
LOG2E = 1.4426950408889634

LRU_WIDTH = 512
LRU_BLOCKS = 8
LRU_BLOCK_DIM = LRU_WIDTH // LRU_BLOCKS
LRU_CONV_W = 4
LRU_C = 8.0
SWA_HEADS = 8
SWA_KV_HEADS = 2
SWA_HEAD_DIM = 64
WINDOW = 128
MLA_HEADS = 8
MLA_Q_RANK = 256
MLA_KV_RANK = 128
MLA_NOPE = 64
MLA_ROPE = 32
MLA_V = 64
CONF_CH = 512
CONF_K = 31
N_GROUPS = 4
EXPERTS_PER_GROUP = 8
N_EXPERTS = N_GROUPS * EXPERTS_PER_GROUP
TOP_K = 2
D_EXPERT = 512

LANES = 128
SUBLANES = 8
ROW_TILE = 256
SCAN_SEG = ROW_TILE // SUBLANES
SCAN_PITCH = SCAN_SEG + 4
CONF_HALO = 16
CONF_SUB = 32
MOE_TILE = 256
DISPATCH_TILES = 4
FLASH_SUB = 256
FLASH_TQ = 1024
VMEM_LIMIT = 48 * 1024 * 1024


def _cparams(n_axes, vmem=VMEM_LIMIT):
    return pltpu.CompilerParams(dimension_semantics=("arbitrary",) * n_axes, vmem_limit_bytes=vmem)


def _dot(a, b):
    return jnp.dot(a, b, preferred_element_type=F32)


def _dot_nt(a, b):
    return lax.dot_general(a, b, (((1,), (1,)), ((), ())), preferred_element_type=F32)


def _split_bf16(a):
    hi = a.astype(BF16)
    lo = (a - hi.astype(F32)).astype(BF16)
    return hi, lo


def _silu(x):
    return x * jax.nn.sigmoid(x)


def _gelu_tanh(x):
    return 0.5 * x * (1.0 + jnp.tanh(np.sqrt(2.0 / np.pi).astype(np.float32) * (x + 0.044715 * (x * x * x))))


def _layer_norm(x, g, b):
    mu = jnp.mean(x, axis=-1, keepdims=True)
    xc = x - mu
    var = jnp.mean(xc * xc, axis=-1, keepdims=True)
    return xc * lax.rsqrt(var + LN_EPS) * g + b


def _select_mod(mb_ref, mc_ref, idx, is_ctx):
    return jnp.where(is_ctx, mc_ref[0, idx:idx + 1, :], mb_ref[0, idx:idx + 1, :])


def _mod_kernel(c_ref, w_ref, b_ref, o_ref):
    c = c_ref[...]
    s_hi, s_lo = _split_bf16(_silu(c))
    w_hi, w_lo = _split_bf16(w_ref[0])
    o_ref[0] = _dot(s_hi, w_hi) + _dot(s_lo, w_hi) + _dot(s_hi, w_lo) + b_ref[0]


def _modulation(cond8, w_mod, b_mod):
    depth, d, n = w_mod.shape
    tn = 1536
    return pl.pallas_call(
        _mod_kernel,
        grid=(depth, n // tn),
        in_specs=[pl.BlockSpec((SUBLANES, d), lambda l, j: (0, 0)),
                  pl.BlockSpec((1, d, tn), lambda l, j: (l, 0, j)),
                  pl.BlockSpec((1, 1, tn), lambda l, j: (l, 0, j))],
        out_specs=pl.BlockSpec((1, SUBLANES, tn), lambda l, j: (l, 0, j)),
        out_shape=jax.ShapeDtypeStruct((depth, SUBLANES, n), F32),
        compiler_params=_cparams(2),
        name="modulation",
    )(cond8, w_mod, b_mod.reshape(depth, 1, n))


def _rope_pattern(width, row, col):
    d = width // 2
    half = d // 2
    inv = ROPE_THETA ** (-jnp.arange(half, dtype=F32) / half)
    ang_r = row.astype(F32)[:, None] * inv[None, :]
    ang_c = col.astype(F32)[:, None] * inv[None, :]
    cos = jnp.concatenate([jnp.cos(ang_r)] * 2 + [jnp.cos(ang_c)] * 2, axis=1)
    sin = jnp.concatenate([-jnp.sin(ang_r), jnp.sin(ang_r), -jnp.sin(ang_c), jnp.sin(ang_c)], axis=1)
    partner = np.concatenate([np.arange(half) + half, np.arange(half), d + np.arange(half) + half, d + np.arange(half)])
    return cos, sin, partner


def _in_even_kernel(x_ref, c_ref, mb_ref, mc_ref, w_ref, b_ref, cos_ref, sin_ref,
                    g_ref, u_ref, q_ref, k_ref, v_ref, *, l_lat):
    i = pl.program_id(1)
    tm = x_ref.shape[1]
    rows = i * tm + lax.broadcasted_iota(jnp.int32, (tm, 1), 0)
    is_ctx = rows >= l_lat
    shift = _select_mod(mb_ref, mc_ref, 0, is_ctx)
    scale = _select_mod(mb_ref, mc_ref, 1, is_ctx)
    x_in = jnp.where(i * tm >= l_lat, c_ref[0], x_ref[0])
    h = (x_in * (1.0 + scale) + shift).astype(BF16)

    def proj(lo, hi):
        return _dot(h, w_ref[:, lo:hi]) + b_ref[:, lo:hi]

    w = LRU_WIDTH
    g_ref[0] = proj(0, w)
    u_ref[0] = proj(w, 2 * w)
    cos = cos_ref[...]
    sin = sin_ref[...]
    cos4 = jnp.concatenate([cos] * 4, axis=1)
    sin4 = jnp.concatenate([sin] * 4, axis=1)
    o = 2 * w
    q = (proj(o, o + 512) * cos4 + proj(o + 512, o + 1024) * sin4) * (SWA_HEAD_DIM ** -0.5 * LOG2E)
    q_ref[0] = q.astype(BF16)
    o += 1024
    cos2 = jnp.concatenate([cos] * 2, axis=1)
    sin2 = jnp.concatenate([sin] * 2, axis=1)
    k = proj(o, o + 256) * cos2 + proj(o + 256, o + 512) * sin2
    k_ref[0] = k.astype(BF16)
    o += 512
    v_ref[0] = proj(o, o + 256).astype(BF16)


def _in_even(x, ctx, mods, w, b, cos, sin):
    bsz, l_lat, d = x.shape
    s_all = l_lat + ctx.shape[1]
    tm = ROW_TILE
    n_lat = l_lat // tm
    nw = w.shape[1]
    row = lambda n: pl.BlockSpec((1, tm, n), lambda bb, i: (bb, i, 0))
    return pl.pallas_call(
        functools.partial(_in_even_kernel, l_lat=l_lat),
        grid=(bsz, s_all // tm),
        in_specs=[pl.BlockSpec((1, tm, d), lambda bb, i: (bb, jnp.minimum(i, n_lat - 1), 0)),
                  pl.BlockSpec((1, tm, d), lambda bb, i: (bb, 0, 0)),
                  pl.BlockSpec((1, 6, d), lambda bb, i: (bb, 0, 0)),
                  pl.BlockSpec((1, 6, d), lambda bb, i: (4, 0, 0)),
                  pl.BlockSpec((d, nw), lambda bb, i: (0, 0)),
                  pl.BlockSpec((1, nw), lambda bb, i: (0, 0)),
                  pl.BlockSpec((tm, LANES), lambda bb, i: (i, 0)),
                  pl.BlockSpec((tm, LANES), lambda bb, i: (i, 0))],
        out_specs=[row(512), row(512), row(512), row(256), row(256)],
        out_shape=[jax.ShapeDtypeStruct((bsz, s_all, 512), F32),
                   jax.ShapeDtypeStruct((bsz, s_all, 512), F32),
                   jax.ShapeDtypeStruct((bsz, s_all, 512), BF16),
                   jax.ShapeDtypeStruct((bsz, s_all, 256), BF16),
                   jax.ShapeDtypeStruct((bsz, s_all, 256), BF16)],
        compiler_params=_cparams(2),
        name="in_even",
    )(x, ctx, mods, mods, w, b, cos, sin)


def _softplus(x):
    return jnp.maximum(x, 0.0) + jnp.log1p(jnp.exp(-jnp.abs(x)))


def _lru_chunk_index(s, n_lat, reverse):
    if reverse:
        return jnp.where(s == 0, n_lat, n_lat - s)
    return jnp.where(s == 0, n_lat, s - 1)


def _lru_kernel(*refs, reverse, n_lat):
    if reverse:
        (u_ref, up_ref, un_ref, cw_ref, cb_ref, w_ref, bias_ref, lam_ref, g_ref, hf_ref,
         o_ref, ext, a_s, b_s, p_s, h_s, carry) = refs
    else:
        (u_ref, up_ref, un_ref, cw_ref, cb_ref, w_ref, bias_ref, lam_ref,
         o_ref, ext, a_s, b_s, p_s, h_s, carry) = refs
    s = pl.program_id(1)
    j = _lru_chunk_index(s, n_lat, reverse)
    t = ROW_TILE
    width = LRU_WIDTH

    @pl.when(s == 0)
    def _():
        carry[...] = jnp.zeros_like(carry)

    prev_ok = jnp.logical_and(j != 0, j != n_lat)
    next_ok = jnp.logical_and(j != n_lat - 1, j != n_lat)
    ext[0:SUBLANES, :] = jnp.where(prev_ok, up_ref[0], 0.0)
    ext[SUBLANES:SUBLANES + t, :] = u_ref[0]
    ext[SUBLANES + t:, :] = jnp.where(next_ok, un_ref[0], 0.0)
    uc = cb_ref[...]
    for kk in range(LRU_CONV_W):
        uc = uc + cw_ref[kk:kk + 1, :] * ext[SUBLANES - 1 + kk:SUBLANES - 1 + kk + t, :]

    z = _dot(uc.astype(BF16), w_ref[...]) + bias_ref[...]
    r = jax.nn.sigmoid(z[:, :width])
    ig = jax.nn.sigmoid(z[:, width:])
    log_a = (-LRU_C) * r * _softplus(-lam_ref[...])
    a = jnp.exp(log_a)
    b = jnp.sqrt(-jnp.tanh(log_a) * (a * a + 1.0)) * (ig * uc)

    steps = range(SCAN_SEG - 1, -1, -1) if reverse else range(SCAN_SEG)
    segs = range(SUBLANES - 1, -1, -1) if reverse else range(SUBLANES)
    blocks = []
    for cb in range(width // LANES):
        cols = slice(cb * LANES, (cb + 1) * LANES)
        for sg in range(SUBLANES):
            a_s[cb, sg * SCAN_PITCH:sg * SCAN_PITCH + SCAN_SEG, :] = a[sg * SCAN_SEG:(sg + 1) * SCAN_SEG, cols]
            b_s[cb, sg * SCAN_PITCH:sg * SCAN_PITCH + SCAN_SEG, :] = b[sg * SCAN_SEG:(sg + 1) * SCAN_SEG, cols]
        h = jnp.zeros((SUBLANES, LANES), F32)
        p = jnp.ones((SUBLANES, LANES), F32)
        for i in steps:
            sl = pl.ds(i, SUBLANES, stride=SCAN_PITCH)
            ai = a_s[cb, sl, :]
            h = ai * h + b_s[cb, sl, :]
            p = p * ai
            h_s[cb, sl, :] = h
            p_s[cb, sl, :] = p
        c = carry[:, cols]
        seg_in = [None] * SUBLANES
        for sg in segs:
            seg_in[sg] = c
            c = p[sg:sg + 1, :] * c + h[sg:sg + 1, :]
        carry[:, cols] = c
        cin = jnp.concatenate(seg_in, axis=0)
        for i in range(SCAN_SEG):
            sl = pl.ds(i, SUBLANES, stride=SCAN_PITCH)
            h_s[cb, sl, :] = h_s[cb, sl, :] + p_s[cb, sl, :] * cin
        blocks.append(jnp.concatenate([h_s[cb, sg * SCAN_PITCH:sg * SCAN_PITCH + SCAN_SEG, :]
                                       for sg in range(SUBLANES)], axis=0))
    hs = jnp.concatenate(blocks, axis=1)
    if reverse:
        o_ref[0] = (_gelu_tanh(g_ref[0]) * (hf_ref[0] + hs)).astype(o_ref.dtype)
    else:
        o_ref[0] = hs


def _lru(u, g, hf, conv_w, conv_b, w, bias, lam, reverse, l_lat):
    bsz, s_all, width = u.shape
    t = ROW_TILE
    n_chunks = s_all // t
    n_lat = l_lat // t
    hb = t // SUBLANES
    n_hb = s_all // SUBLANES
    cidx = functools.partial(_lru_chunk_index, n_lat=n_lat, reverse=reverse)
    chunk = pl.BlockSpec((1, t, width), lambda bb, s: (bb, cidx(s), 0))
    const = lambda shape: pl.BlockSpec(shape, lambda bb, s: (0,) * len(shape))
    in_specs = [chunk,
                pl.BlockSpec((1, SUBLANES, width), lambda bb, s: (bb, jnp.maximum(cidx(s) * hb - 1, 0), 0)),
                pl.BlockSpec((1, SUBLANES, width), lambda bb, s: (bb, jnp.minimum(cidx(s) * hb + hb, n_hb - 1), 0)),
                const((LRU_CONV_W, width)), const((1, width)), const((width, 2 * width)),
                const((1, 2 * width)), const((1, width))]
    args = [u, u, u, conv_w, conv_b, w, bias, lam]
    if reverse:
        in_specs += [chunk, chunk]
        args += [g, hf]
    return pl.pallas_call(
        functools.partial(_lru_kernel, reverse=reverse, n_lat=n_lat),
        grid=(bsz, n_chunks),
        in_specs=in_specs,
        out_specs=chunk,
        out_shape=jax.ShapeDtypeStruct((bsz, s_all, width), BF16 if reverse else F32),
        scratch_shapes=[pltpu.VMEM((t + 2 * SUBLANES, width), F32)]
                       + [pltpu.VMEM((width // LANES, SUBLANES * SCAN_PITCH, LANES), F32)] * 4
                       + [pltpu.VMEM((1, width), F32)],
        compiler_params=_cparams(2),
        name="lru_bwd" if reverse else "lru_fwd",
    )(*args)


def _swa_kernel(sink_ref, q_ref, k_ref, v_ref, o_ref, *, l_lat, n_ctx):
    i = pl.program_id(1)
    tq = q_ref.shape[1]
    wk = tq + 2 * WINDOW
    is_lat = i < l_lat // tq
    ws = pl.multiple_of(jnp.clip(i * tq - WINDOW, 0, l_lat - wk), WINDOW)
    qpos = i * tq + lax.broadcasted_iota(jnp.int32, (tq, 1), 0)
    kpos = ws + lax.broadcasted_iota(jnp.int32, (1, wk), 1)
    band = jnp.logical_and(jnp.abs(qpos - kpos) <= WINDOW, is_lat)
    valid = jnp.concatenate([band, jnp.ones((tq, n_ctx), jnp.bool_)], axis=1)
    lo = lax.broadcasted_iota(jnp.int32, (1, LANES), 1) < SWA_HEAD_DIM
    heads_per_kv = SWA_HEADS // SWA_KV_HEADS
    keys, vals = [], []
    for kh in range(SWA_KV_HEADS):
        cols = slice(LANES * kh, LANES * (kh + 1))
        keys.append(jnp.concatenate([k_ref[0, pl.ds(ws, wk), cols], k_ref[0, l_lat:l_lat + n_ctx, cols]], axis=0))
        vals.append(jnp.concatenate([v_ref[0, pl.ds(ws, wk), cols], v_ref[0, l_lat:l_lat + n_ctx, cols]], axis=0))
    for kh in range(SWA_KV_HEADS):
        heads = range(kh * heads_per_kv, (kh + 1) * heads_per_kv)
        scores = []
        for h in heads:
            qb = q_ref[0, :, LANES * (h // 2):LANES * (h // 2 + 1)]
            qm = jnp.where(lo if h % 2 == 0 else jnp.logical_not(lo), qb, jnp.zeros_like(qb))
            scores.append(_dot_nt(qm, keys[kh]))
        probs = []
        for h, sc in zip(heads, scores):
            s = jnp.where(valid, sc, NEG_INF)
            sink = sink_ref[h] * LOG2E
            m = jnp.maximum(jnp.max(s, axis=-1, keepdims=True), sink)
            p = jnp.exp2(s - m)
            probs.append((p.astype(BF16), jnp.sum(p, axis=-1, keepdims=True) + jnp.exp2(sink - m)))
        outs = [_dot(p, vals[kh]) / den for p, den in probs]
        for jq in range(heads_per_kv // 2):
            blk = kh * (heads_per_kv // 2) + jq
            o_ref[0, :, LANES * blk:LANES * (blk + 1)] = jnp.where(lo, outs[2 * jq], outs[2 * jq + 1]).astype(o_ref.dtype)


def _swa(q, k2, v2, sink, l_lat):
    bsz, s_all, _ = q.shape
    tq = ROW_TILE
    n_ctx = s_all - l_lat
    return pl.pallas_call(
        functools.partial(_swa_kernel, l_lat=l_lat, n_ctx=n_ctx),
        grid_spec=pltpu.PrefetchScalarGridSpec(
            num_scalar_prefetch=1,
            grid=(bsz, s_all // tq),
            in_specs=[pl.BlockSpec((1, tq, 512), lambda bb, i, s: (bb, i, 0)),
                      pl.BlockSpec((1, s_all, 256), lambda bb, i, s: (bb, 0, 0)),
                      pl.BlockSpec((1, s_all, 256), lambda bb, i, s: (bb, 0, 0))],
            out_specs=pl.BlockSpec((1, tq, 512), lambda bb, i, s: (bb, i, 0))),
        out_shape=jax.ShapeDtypeStruct((bsz, s_all, 512), BF16),
        compiler_params=_cparams(2),
        name="swa",
    )(sink, q, k2, v2)


def _route(logits):
    lane = lax.broadcasted_iota(jnp.int32, logits.shape, 1)
    lane_f = lane.astype(F32)

    def first_argmax(vals, mask):
        v = jnp.where(mask, vals, NEG_INF)
        m = jnp.max(v, axis=-1, keepdims=True)
        idx = jnp.min(jnp.where(jnp.logical_and(mask, v == m), lane_f, float(LANES)), axis=-1, keepdims=True)
        return m, idx.astype(jnp.int32)

    gmask = lane < N_GROUPS
    gm, gidx = first_argmax(logits, gmask)
    g_prob = 1.0 / jnp.sum(jnp.where(gmask, jnp.exp(logits - gm), 0.0), axis=-1, keepdims=True)
    base = N_GROUPS + gidx * EXPERTS_PER_GROUP
    emask = jnp.logical_and(lane >= base, lane < base + EXPERTS_PER_GROUP)
    m0, i0 = first_argmax(logits, emask)
    m1, i1 = first_argmax(logits, jnp.logical_and(emask, lane != i0))
    w1 = jnp.exp(m1 - m0)
    inv = g_prob / (1.0 + w1)
    return i0 - N_GROUPS, i1 - N_GROUPS, inv, w1 * inv


TOKEN_SUB = D_MODEL // LANES


def _store_token_rows(ref, val):
    n = val.shape[0]
    for j in range(TOKEN_SUB):
        ref[pl.ds(j, n, stride=TOKEN_SUB), :] = val[:, j * LANES:(j + 1) * LANES]


def _load_token_rows(ref, n):
    return jnp.concatenate([ref[pl.ds(j, n, stride=TOKEN_SUB), :] for j in range(TOKEN_SUB)], axis=1)


def _post_mix_kernel(a_ref, b_ref, x_ref, c_ref, mb_ref, mc_ref, w_ref, bo_ref, lng_ref, lnb_ref, wr_ref, br_ref,
                     x1_ref, f_ref, r_ref, cnt_ref, counts, *, l_lat):
    i = pl.program_id(1)
    tm = x_ref.shape[1]
    rows = i * tm + lax.broadcasted_iota(jnp.int32, (tm, 1), 0)
    is_ctx = rows >= l_lat
    half = a_ref.shape[2]
    y = _dot(a_ref[0], w_ref[0:half, :]) + _dot(b_ref[0], w_ref[half:, :]) + bo_ref[...]
    gate = _select_mod(mb_ref, mc_ref, 2, is_ctx)
    x_in = jnp.where(i * tm >= l_lat, c_ref[0], x_ref[0])
    x1 = _layer_norm(DN_ALPHA * x_in + gate * y, lng_ref[...], lnb_ref[...])
    x1_ref[0] = x1
    f = x1 * (1.0 + _select_mod(mb_ref, mc_ref, 4, is_ctx)) + _select_mod(mb_ref, mc_ref, 3, is_ctx)
    _store_token_rows(f_ref.at[0], f)
    f_hi, f_lo = _split_bf16(f)
    two = _dot(f_hi, wr_ref[...])
    logits = two[:, :LANES] + two[:, LANES:] + _dot(f_lo, wr_ref[:, :LANES]) + br_ref[...]
    e0, e1, g0, g1 = _route(logits)

    @pl.when(jnp.logical_and(pl.program_id(0) == 0, i == 0))
    def _():
        counts[...] = jnp.zeros_like(counts)

    lane = lax.broadcasted_iota(jnp.int32, (tm, LANES), 1)
    hot0 = lane == e0
    hot1 = lane == e1
    both = jnp.where(hot0, 1.0, 0.0) + jnp.where(hot1, 1.0, 0.0)
    tri = (lax.broadcasted_iota(jnp.int32, (tm, tm), 1) < lax.broadcasted_iota(jnp.int32, (tm, tm), 0))
    before = _dot(jnp.where(tri, 1.0, 0.0).astype(BF16), both.astype(BF16)) + counts[...]
    r0 = jnp.sum(jnp.where(hot0, before, 0.0), axis=-1, keepdims=True)
    r1 = jnp.sum(jnp.where(hot1, before, 0.0), axis=-1, keepdims=True)
    counts[...] = counts[...] + jnp.sum(both, axis=0, keepdims=True)
    cnt_ref[...] = counts[...]
    r_ref[0] = jnp.where(lane == 0, e0.astype(F32), jnp.where(lane == 1, e1.astype(F32), jnp.where(
        lane == 2, g0, jnp.where(lane == 3, g1, jnp.where(lane == 4, r0, jnp.where(lane == 5, r1, 0.0))))))


def _post_mix(a, b, x, ctx, mods, w_out, b_out, ln_g, ln_b, w_r, b_r, n_rows, l_lat):
    bsz = x.shape[0]
    d = x.shape[2]
    tm = ROW_TILE
    half = a.shape[2]
    n_lat = l_lat // tm
    row = lambda n: pl.BlockSpec((1, tm, n), lambda bb, i: (bb, i, 0))
    const = lambda shape: pl.BlockSpec(shape, lambda bb, i: (0,) * len(shape))
    return pl.pallas_call(
        functools.partial(_post_mix_kernel, l_lat=l_lat),
        grid=(bsz, n_rows // tm),
        in_specs=[row(half), row(half),
                  pl.BlockSpec((1, tm, d), lambda bb, i: (bb, jnp.minimum(i, n_lat - 1), 0)),
                  pl.BlockSpec((1, tm, d), lambda bb, i: (bb, 0, 0)),
                  pl.BlockSpec((1, 6, d), lambda bb, i: (bb, 0, 0)),
                  pl.BlockSpec((1, 6, d), lambda bb, i: (4, 0, 0)),
                  const((2 * half, d)), const((1, d)), const((1, d)), const((1, d)),
                  const((d, 2 * LANES)), const((1, LANES))],
        out_specs=[row(d), pl.BlockSpec((1, tm * TOKEN_SUB, LANES), lambda bb, i: (bb, i, 0)), row(LANES),
                   const((1, LANES))],
        out_shape=[jax.ShapeDtypeStruct((bsz, n_rows, d), F32),
                   jax.ShapeDtypeStruct((bsz, n_rows * TOKEN_SUB, LANES), F32),
                   jax.ShapeDtypeStruct((bsz, n_rows, LANES), F32),
                   jax.ShapeDtypeStruct((1, LANES), F32)],
        scratch_shapes=[pltpu.VMEM((1, LANES), F32)],
        compiler_params=_cparams(2),
        name="post_mix",
    )(a, b, x, ctx, mods, mods, w_out, b_out, ln_g, ln_b, w_r, b_r)


def _token_rows(ref, tok, n=1):
    start = tok * TOKEN_SUB
    if not isinstance(start, int):
        start = pl.multiple_of(start, TOKEN_SUB)
    return ref.at[pl.ds(start, n * TOKEN_SUB), :]


def _dispatch_kernel(pad_ref, nb_ref, dest_ref, f_ref, xb_ref, zeros, sem_z, sem):
    i = pl.program_id(0)
    tiles = dest_ref.shape[0]
    tm = f_ref.shape[0] // TOKEN_SUB // tiles
    tb = zeros.shape[0] // TOKEN_SUB
    n_blocks = xb_ref.shape[0] // zeros.shape[0]

    @pl.when(i == 0)
    def _():
        zeros[...] = jnp.zeros_like(zeros)
        fills = [pltpu.make_async_copy(zeros, _token_rows(xb_ref, pad_ref[e], tb), sem_z) for e in range(N_EXPERTS)]
        for cp in fills:
            cp.start()
        for cp in fills:
            cp.wait()

        def tail(b):
            return pltpu.make_async_copy(zeros, _token_rows(xb_ref, b * tb, tb), sem_z)

        def start(b, carry):
            tail(b).start()
            return carry

        def wait(b, carry):
            tail(b).wait()
            return carry

        lax.fori_loop(nb_ref[0], n_blocks, start, 0)
        lax.fori_loop(nb_ref[0], n_blocks, wait, 0)

    copies = [pltpu.make_async_copy(_token_rows(f_ref, t * tm + r),
                                    _token_rows(xb_ref, dest_ref[t, 0, kk * tm + r]), sem)
              for t in range(tiles) for r in range(tm) for kk in range(TOP_K)]
    for n, cp in enumerate(copies):
        cp.start(priority=n % 2)
    for cp in copies:
        cp.wait()


def _dispatch(pad_start, n_used, dest_tiles, tokens, n_slots):
    n_tiles = dest_tiles.shape[0]
    tm = ROW_TILE
    tb = MOE_TILE
    tiles = next(t for t in (DISPATCH_TILES, 2, 1) if n_tiles % t == 0)
    return pl.pallas_call(
        _dispatch_kernel,
        grid_spec=pltpu.PrefetchScalarGridSpec(
            num_scalar_prefetch=2,
            grid=(n_tiles // tiles,),
            in_specs=[pl.BlockSpec((tiles, 1, TOP_K * tm), lambda i, pad, nb: (i, 0, 0), memory_space=pltpu.SMEM),
                      pl.BlockSpec((tiles * tm * TOKEN_SUB, LANES), lambda i, pad, nb: (i, 0))],
            out_specs=pl.BlockSpec(memory_space=pl.ANY),
            scratch_shapes=[pltpu.VMEM((tb * TOKEN_SUB, LANES), F32), pltpu.SemaphoreType.DMA,
                            pltpu.SemaphoreType.DMA]),
        out_shape=jax.ShapeDtypeStruct(((n_slots + tb) * TOKEN_SUB, LANES), F32),
        compiler_params=pltpu.CompilerParams(dimension_semantics=("arbitrary",), vmem_limit_bytes=VMEM_LIMIT,
                                             disable_bounds_checks=True),
        name="moe_dispatch",
    )(pad_start, n_used, dest_tiles, tokens)


def _expert_kernel(be_ref, nb_ref, x_ref, w1_ref, w3_ref, w2_ref, o_ref, w1b, w3b, w2b):
    i = pl.program_id(0)
    prev = be_ref[jnp.maximum(i - 1, 0)]

    @pl.when(jnp.logical_or(i == 0, be_ref[i] != prev))
    def _():
        w1b[...] = w1_ref[0, 0].astype(BF16)
        w3b[...] = w3_ref[0, 0].astype(BF16)
        w2b[...] = w2_ref[0, 0].astype(BF16)

    @pl.when(i < nb_ref[0])
    def _():
        xb = _load_token_rows(x_ref, x_ref.shape[0] // TOKEN_SUB).astype(BF16)
        h1 = _dot(xb, w1b[...])
        h3 = _dot(xb, w3b[...])
        act = (_silu(h1) * h3).astype(BF16)
        _store_token_rows(o_ref, _dot(act, w2b[...]))

    @pl.when(i >= nb_ref[0])
    def _():
        o_ref[...] = jnp.zeros_like(o_ref)


def _experts(block_expert, n_used, xb, w1, w3, w2, layer):
    tb = MOE_TILE
    n_blocks = block_expert.shape[0]
    _, _, d, de = w1.shape
    last = lambda i, nb: jnp.minimum(i, jnp.maximum(nb[0] - 1, 0))
    return pl.pallas_call(
        _expert_kernel,
        grid_spec=pltpu.PrefetchScalarGridSpec(
            num_scalar_prefetch=2,
            grid=(n_blocks,),
            in_specs=[pl.BlockSpec((tb * TOKEN_SUB, LANES), lambda i, be, nb: (last(i, nb), 0)),
                      pl.BlockSpec((1, 1, d, de), lambda i, be, nb: (layer, be[i], 0, 0)),
                      pl.BlockSpec((1, 1, d, de), lambda i, be, nb: (layer, be[i], 0, 0)),
                      pl.BlockSpec((1, 1, de, d), lambda i, be, nb: (layer, be[i], 0, 0))],
            out_specs=pl.BlockSpec((tb * TOKEN_SUB, LANES), lambda i, be, nb: (i, 0)),
            scratch_shapes=[pltpu.VMEM((d, de), BF16), pltpu.VMEM((d, de), BF16), pltpu.VMEM((de, d), BF16)]),
        out_shape=jax.ShapeDtypeStruct((n_blocks * tb * TOKEN_SUB, LANES), F32),
        compiler_params=_cparams(1),
        name="experts",
    )(block_expert, n_used, xb, w1, w3, w2)


def _post_ffn_kernel(dcur_ref, dnxt_ref, x_ref, r_ref, yb_ref, mb_ref, mc_ref, lng_ref, lnb_ref, o_ref, ybuf, sems,
                     *, l_lat):
    bb, i = pl.program_id(0), pl.program_id(1)
    nt = pl.num_programs(1)
    step = bb * nt + i
    n_steps = pl.num_programs(0) * nt
    tm = x_ref.shape[1]

    def gathers(dref, slot):
        return [pltpu.make_async_copy(_token_rows(yb_ref, dref[0, 0, kk * tm + r]),
                                      _token_rows(ybuf.at[slot, kk], r), sems.at[slot])
                for r in range(tm) for kk in range(TOP_K)]

    def issue(dref, slot):
        for n, cp in enumerate(gathers(dref, slot)):
            cp.start(priority=n % 2)

    cur = step % 2

    @pl.when(step == 0)
    def _():
        issue(dcur_ref, 0)

    @pl.when(step + 1 < n_steps)
    def _():
        issue(dnxt_ref, 1 - cur)

    for cp in gathers(dcur_ref, cur):
        cp.wait()

    rows = i * tm + lax.broadcasted_iota(jnp.int32, (tm, 1), 0)
    is_ctx = rows >= l_lat
    gate = _select_mod(mb_ref, mc_ref, 5, is_ctx)
    r = r_ref[0]
    ffn = r[:, 2:3] * _load_token_rows(ybuf.at[cur, 0], tm) + r[:, 3:4] * _load_token_rows(ybuf.at[cur, 1], tm)
    o_ref[0] = _layer_norm(DN_ALPHA * x_ref[0] + gate * ffn, lng_ref[...], lnb_ref[...])


def _post_ffn(x1, route, dest_tiles, yb, mods, ln_g, ln_b, l_lat):
    bsz, n_rows, d = x1.shape
    tm = ROW_TILE
    nt = n_rows // tm
    n_steps = bsz * nt
    row = lambda n: pl.BlockSpec((1, tm, n), lambda bb, i: (bb, i, 0))
    const = lambda shape: pl.BlockSpec(shape, lambda bb, i: (0,) * len(shape))
    dspec = lambda ahead: pl.BlockSpec((1, 1, TOP_K * tm),
                                       lambda bb, i: (jnp.minimum(bb * nt + i + ahead, n_steps - 1), 0, 0),
                                       memory_space=pltpu.SMEM)
    return pl.pallas_call(
        functools.partial(_post_ffn_kernel, l_lat=l_lat),
        grid=(bsz, nt),
        in_specs=[dspec(0), dspec(1), row(d), row(LANES), pl.BlockSpec(memory_space=pl.ANY),
                  pl.BlockSpec((1, 6, d), lambda bb, i: (bb, 0, 0)),
                  pl.BlockSpec((1, 6, d), lambda bb, i: (4, 0, 0)),
                  const((1, d)), const((1, d))],
        out_specs=row(d),
        out_shape=jax.ShapeDtypeStruct((bsz, n_rows, d), F32),
        scratch_shapes=[pltpu.VMEM((2, TOP_K, tm * TOKEN_SUB, LANES), F32), pltpu.SemaphoreType.DMA((2,))],
        compiler_params=pltpu.CompilerParams(dimension_semantics=("arbitrary", "arbitrary"),
                                             vmem_limit_bytes=VMEM_LIMIT, disable_bounds_checks=True),
        name="post_ffn",
    )(dest_tiles, dest_tiles, x1, route, yb, mods, mods, ln_g, ln_b)


def _moe(f, route, counts, w1, w3, w2, layer):
    bsz, n_rows, _ = route.shape
    n_tok = bsz * n_rows
    tb = MOE_TILE
    tm = ROW_TILE
    r = route.reshape(n_tok, LANES)
    expert = r[:, 0:TOP_K].astype(jnp.int32)
    rank = r[:, 2 * TOP_K:3 * TOP_K].astype(jnp.int32)
    cnt = counts[0, :N_EXPERTS].astype(jnp.int32)
    padded = (cnt + tb - 1) // tb * tb
    p_ends = jnp.cumsum(padded)
    p_starts = p_ends - padded
    ids = jnp.arange(N_EXPERTS, dtype=jnp.int32)
    dest = jnp.sum(jnp.where(expert[:, :, None] == ids[None, None, :], p_starts[None, None, :], 0), axis=-1) + rank
    dest_tiles = dest.reshape(n_tok // tm, tm, TOP_K).transpose(0, 2, 1).reshape(n_tok // tm, 1, TOP_K * tm)
    n_blocks = -(-(n_tok * TOP_K + N_EXPERTS * (tb - 1)) // tb)
    block_start = jnp.arange(n_blocks, dtype=jnp.int32) * tb
    block_expert = jnp.minimum(jnp.sum((p_ends[None, :] <= block_start[:, None]).astype(jnp.int32), axis=1),
                               N_EXPERTS - 1)
    n_used = (p_ends[-1] // tb).reshape(1)
    xb = _dispatch(p_starts + cnt, n_used, dest_tiles, f.reshape(n_tok * TOKEN_SUB, LANES), n_blocks * tb)
    return _experts(block_expert, n_used, xb, w1, w3, w2, layer), dest_tiles


def _rms(x, g):
    return x * lax.rsqrt(jnp.mean(x * x, axis=-1, keepdims=True) + LN_EPS) * g

def _in_odd_kernel(x_ref, mb_ref, mc_ref, w_ref, b_ref, qn_ref, kvn_ref, wuq_ref, wuk_ref, place_ref, wuvt_ref,
                   cq_ref, sq_ref, ck_ref, sk_ref, q_ref, k_ref, vt_ref, glu_ref, *, l_lat):
    i = pl.program_id(0)
    tm = x_ref.shape[1]
    rows = i * tm + lax.broadcasted_iota(jnp.int32, (tm, 1), 0)
    is_ctx = rows >= l_lat
    shift = _select_mod(mb_ref, mc_ref, 0, is_ctx)
    scale = _select_mod(mb_ref, mc_ref, 1, is_ctx)
    h = (x_ref[0] * (1.0 + scale) + shift).astype(BF16)

    def proj(lo, hi):
        return _dot(h, w_ref[:, lo:hi]) + b_ref[:, lo:hi]

    cqn = _rms(proj(0, MLA_Q_RANK), qn_ref[...]).astype(BF16)
    hw = MLA_HEADS * LANES
    cos8 = jnp.concatenate([cq_ref[...]] * MLA_HEADS, axis=1)
    sin8 = jnp.concatenate([sq_ref[...]] * MLA_HEADS, axis=1)
    q = _dot(cqn, wuq_ref[:, :hw]) * cos8 + _dot(cqn, wuq_ref[:, hw:]) * sin8
    q_ref[0] = (q * ((MLA_NOPE + MLA_ROPE) ** -0.5 * LOG2E)).astype(BF16)
    o = MLA_Q_RANK
    cn = _rms(proj(o, o + MLA_KV_RANK), kvn_ref[...]).astype(BF16)
    o += MLA_KV_RANK
    kpe = proj(o, o + LANES) * ck_ref[...] + proj(o + LANES, o + 2 * LANES) * sk_ref[...]
    k_ref[0] = (_dot(cn, wuk_ref[...]) + _dot(kpe.astype(BF16), place_ref[...])).astype(BF16)
    vt = _dot_nt(wuvt_ref[...], cn)
    vt_ref[0] = vt.reshape(MLA_HEADS, 1, MLA_V, tm).astype(BF16)
    o += 2 * LANES
    glu_ref[0] = proj(o, o + CONF_CH) * jax.nn.sigmoid(proj(o + CONF_CH, o + 2 * CONF_CH))


def _in_odd(x_all, mods, w, b, qn, kvn, wuq, wuk, place, wuvt, cq, sq, ck, sk, l_lat, tm):
    bsz, s_all, d = x_all.shape
    nw = w.shape[1]
    nk = s_all // tm
    hw = MLA_HEADS * LANES
    row = lambda n: pl.BlockSpec((1, tm, n), lambda i, bb: (bb, i, 0))
    const = lambda shape: pl.BlockSpec(shape, lambda i, bb: (0,) * len(shape))
    tab = pl.BlockSpec((tm, LANES), lambda i, bb: (i, 0))
    return pl.pallas_call(
        functools.partial(_in_odd_kernel, l_lat=l_lat),
        grid=(nk, bsz),
        in_specs=[row(d),
                  pl.BlockSpec((1, 6, d), lambda i, bb: (bb, 0, 0)),
                  pl.BlockSpec((1, 6, d), lambda i, bb: (4, 0, 0)),
                  const((d, nw)), const((1, nw)), const((1, MLA_Q_RANK)), const((1, MLA_KV_RANK)),
                  const((MLA_Q_RANK, 2 * hw)), const((MLA_KV_RANK, hw)), const((LANES, hw)),
                  const((MLA_HEADS * MLA_V, MLA_KV_RANK)), tab, tab, tab, tab],
        out_specs=[row(hw), row(hw),
                   pl.BlockSpec((1, MLA_HEADS, 1, MLA_V, tm), lambda i, bb: (bb, 0, i, 0, 0)),
                   row(CONF_CH)],
        out_shape=[jax.ShapeDtypeStruct((bsz, s_all, hw), BF16),
                   jax.ShapeDtypeStruct((bsz, s_all, hw), BF16),
                   jax.ShapeDtypeStruct((bsz, MLA_HEADS, nk, MLA_V, tm), BF16),
                   jax.ShapeDtypeStruct((bsz, s_all, CONF_CH), F32)],
        compiler_params=_cparams(2),
        name="in_odd",
    )(x_all, mods, mods, w, b, qn, kvn, wuq, wuk, place, wuvt, cq, sq, ck, sk)


def _flash_kernel(q_ref, qn_ref, k_ref, vt_ref, o_ref, s_scr, *, tk, nk):
    tq = q_ref.shape[1]
    chains = [(hh, j) for hh in range(2) for j in range(tq // FLASH_SUB)]

    def scores(c, slot, src=q_ref):
        off = c * tk if isinstance(c, int) else pl.multiple_of(c * tk, tk)
        for ci, (hh, j) in enumerate(chains):
            q = src[0, j * FLASH_SUB:(j + 1) * FLASH_SUB, LANES * hh:LANES * (hh + 1)]
            k = k_ref[0, pl.ds(off, tk), LANES * hh:LANES * (hh + 1)]
            s_scr[slot, ci] = _dot_nt(k, q)

    def update(c, slot, carry):
        probs = []
        for ci, (m, l, acc) in enumerate(carry):
            m_new = jnp.maximum(m, jnp.max(s_scr[slot, ci], axis=0, keepdims=True))
            alpha = jnp.exp2(m - m_new)
            p = jnp.exp2(s_scr[slot, ci] - m_new)
            probs.append((m_new, alpha, alpha * l + jnp.sum(p, axis=0, keepdims=True), p.astype(BF16)))
        out = []
        for (hh, j), (m_new, alpha, l, p), (_, _, acc) in zip(chains, probs, carry):
            out.append((m_new, l, alpha * acc + _dot(vt_ref[0, hh, c], p)))
        return tuple(out)

    def pair(i, carry):
        c = 2 * i + 1
        scores(c + 1, 0)
        carry = update(c, 1, carry)
        scores(c + 2, 1)
        return update(c + 1, 0, carry)

    init = tuple((jnp.full((1, FLASH_SUB), NEG_INF, F32), jnp.zeros((1, FLASH_SUB), F32),
                  jnp.zeros((MLA_V, FLASH_SUB), F32)) for _ in chains)

    @pl.when(pl.program_id(2) == 0)
    def _():
        scores(0, 2)

    if nk >= 2:
        scores(1, 1)
    res = update(0, 2, init)
    n_pairs = max(nk - 2, 0) // 2
    res = lax.fori_loop(0, n_pairs, pair, res)
    c = 2 * n_pairs + 1
    if nk - c == 2:
        scores(c + 1, 0)
        res = update(c, 1, res)
        scores(0, 2, qn_ref)
        res = update(c + 1, 0, res)
    elif nk - c == 1:
        scores(0, 2, qn_ref)
        res = update(c, 1, res)
    else:
        scores(0, 2, qn_ref)
    norm = {ch: acc / l for ch, (m, l, acc) in zip(chains, res)}
    for j in range(tq // FLASH_SUB):
        pair = jnp.concatenate([norm[(0, j)], norm[(1, j)]], axis=0)
        o_ref[0, j * FLASH_SUB:(j + 1) * FLASH_SUB, :] = pair.T.astype(o_ref.dtype)


def _flash(q, k, vt, l_lat, tk):
    bsz, s_all, _ = q.shape
    nk = s_all // tk
    tq = FLASH_TQ if l_lat % FLASH_TQ == 0 else FLASH_SUB
    nt = l_lat // tq
    return pl.pallas_call(
        functools.partial(_flash_kernel, tk=tk, nk=nk),
        grid=(bsz, MLA_HEADS // 2, nt),
        in_specs=[pl.BlockSpec((1, tq, 2 * LANES), lambda bb, hp, i: (bb, i, hp)),
                  pl.BlockSpec((1, tq, 2 * LANES), lambda bb, hp, i: (bb, jnp.minimum(i + 1, nt - 1), hp)),
                  pl.BlockSpec((1, s_all, 2 * LANES), lambda bb, hp, i: (bb, 0, hp)),
                  pl.BlockSpec((1, 2, nk, MLA_V, tk), lambda bb, hp, i: (bb, hp, 0, 0, 0))],
        out_specs=pl.BlockSpec((1, tq, LANES), lambda bb, hp, i: (bb, i, hp)),
        out_shape=jax.ShapeDtypeStruct((bsz, l_lat, MLA_HEADS * MLA_V), BF16),
        scratch_shapes=[pltpu.VMEM((3, 2 * tq // FLASH_SUB, tk, FLASH_SUB), F32)],
        compiler_params=_cparams(3),
        name="mla_attention",
    )(q, q, k, vt)


def _conf_kernel(x_ref, xp_ref, xn_ref, w_ref, b_ref, g_ref, bb_ref, o_ref, ext, shifted):
    i = pl.program_id(1)
    n = pl.num_programs(1)
    t = x_ref.shape[1]
    hl = CONF_HALO
    ext[0:hl, :] = jnp.where(i > 0, xp_ref[0], 0.0)
    ext[hl:hl + t, :] = x_ref[0]
    ext[hl + t:, :] = jnp.where(i < n - 1, xn_ref[0], 0.0)
    span = shifted.shape[1]
    for s in range(SUBLANES):
        shifted[s] = ext[s:s + span, :]
    off = hl - CONF_K // 2
    for r in range(t // CONF_SUB):
        acc = jnp.zeros((CONF_SUB, CONF_CH), F32) + b_ref[...]
        for kk in range(CONF_K):
            st = off + kk
            base = st // SUBLANES * SUBLANES + r * CONF_SUB
            acc = acc + w_ref[kk:kk + 1, :] * shifted[st % SUBLANES, base:base + CONF_SUB, :]
        o_ref[0, r * CONF_SUB:(r + 1) * CONF_SUB, :] = _silu(_layer_norm(acc, g_ref[...], bb_ref[...])).astype(o_ref.dtype)


def _conformer(glu, w, b, g, beta, l_lat):
    bsz, _, ch = glu.shape
    t = ROW_TILE
    hl = CONF_HALO
    hb = t // hl
    n_hb = l_lat // hl
    const = lambda shape: pl.BlockSpec(shape, lambda bb, i: (0,) * len(shape))
    return pl.pallas_call(
        _conf_kernel,
        grid=(bsz, l_lat // t),
        in_specs=[pl.BlockSpec((1, t, ch), lambda bb, i: (bb, i, 0)),
                  pl.BlockSpec((1, hl, ch), lambda bb, i: (bb, jnp.maximum(i * hb - 1, 0), 0)),
                  pl.BlockSpec((1, hl, ch), lambda bb, i: (bb, jnp.minimum(i * hb + hb, n_hb - 1), 0)),
                  const((CONF_K, ch)), const((1, ch)), const((1, ch)), const((1, ch))],
        out_specs=pl.BlockSpec((1, t, ch), lambda bb, i: (bb, i, 0)),
        out_shape=jax.ShapeDtypeStruct((bsz, l_lat, ch), BF16),
        scratch_shapes=[pltpu.VMEM((t + 2 * hl, ch), F32),
                        pltpu.VMEM((SUBLANES, t + 2 * hl - SUBLANES, ch), F32)],
        compiler_params=_cparams(2),
        name="conformer_conv",
    )(glu, glu, glu, w, b, g, beta)


def _block_diag(w):
    nb, c, d = w.shape
    out = jnp.zeros((nb, c, nb, d), w.dtype)
    out = out.at[jnp.arange(nb), :, jnp.arange(nb), :].set(w)
    return out.reshape(nb * c, nb * d)


def _router_weights(w_group, b_group, w_router, b_router):
    d = w_group.shape[0]
    w = jnp.zeros((d, LANES), F32).at[:, :N_GROUPS].set(w_group).at[:, N_GROUPS:N_GROUPS + N_EXPERTS].set(w_router)
    hi, lo = _split_bf16(w)
    b = jnp.zeros((1, LANES), F32).at[0, :N_GROUPS].set(b_group).at[0, N_GROUPS:N_GROUPS + N_EXPERTS].set(b_router)
    return jnp.concatenate([hi, lo], axis=1), b


def kernel(x, c, ctx, c_ctx, w_mod, b_mod, ln_g, ln_b, e_w_in, e_b_in, e_conv_w, e_conv_b, e_lru_wa, e_lru_ba, e_lru_wx, e_lru_bx, e_lru_lambda, e_sink, e_w_out, e_b_out, o_w_in, o_b_in, o_q_norm, o_kv_norm, o_w_uq, o_w_uk, o_w_uv, o_dw_w, o_dw_b, o_cln_g, o_cln_b, o_w_out, o_b_out, moe_w_group, moe_b_group, moe_w_router, moe_b_router, moe_w1, moe_w3, moe_w2):
    bsz, l_lat, d = x.shape
    n_ctx = ctx.shape[1]
    s_all = l_lat + n_ctx
    assert d == D_MODEL and bsz <= 4 and l_lat % ROW_TILE == 0 and n_ctx == ROW_TILE and l_lat % GRID_W == 0
    assert w_mod.shape[0] == DEPTH == 2

    cond8 =jnp.zeros((SUBLANES, d), F32).at[:bsz].set(c).at[4].set(c_ctx)
    mods = _modulation(cond8, w_mod, b_mod).reshape(DEPTH, SUBLANES, 6, d)

    t_idx = jnp.arange(s_all, dtype=jnp.int32)
    row_pos = jnp.where(t_idx < l_lat, t_idx // GRID_W, 0)
    col_pos = jnp.where(t_idx < l_lat, t_idx % GRID_W, 0)

    cos64, sin64, partner64 = _rope_pattern(SWA_HEAD_DIM, row_pos, col_pos)
    cos_e = jnp.concatenate([cos64] * 2, axis=1)
    sin_e = jnp.concatenate([sin64] * 2, axis=1)
    w_in, b_in = e_w_in[0], e_b_in[0]
    oq = 2 * LRU_WIDTH
    ok = oq + SWA_HEADS * SWA_HEAD_DIM
    ov = ok + SWA_KV_HEADS * SWA_HEAD_DIM
    q_cols = oq + np.arange(SWA_HEADS * SWA_HEAD_DIM)
    qp_cols = oq + (np.arange(SWA_HEADS)[:, None] * SWA_HEAD_DIM + partner64[None, :]).reshape(-1)
    dup = np.repeat(np.arange(SWA_KV_HEADS), 2)
    k_cols = ok + (dup[:, None] * SWA_HEAD_DIM + np.arange(SWA_HEAD_DIM)[None, :]).reshape(-1)
    kp_cols = ok + (dup[:, None] * SWA_HEAD_DIM + partner64[None, :]).reshape(-1)
    v_cols = ov + (dup[:, None] * SWA_HEAD_DIM + np.arange(SWA_HEAD_DIM)[None, :]).reshape(-1)
    cols = np.concatenate([np.arange(oq), q_cols, qp_cols, k_cols, kp_cols, v_cols])
    g0, u0, q0, k0, v0 = _in_even(x, ctx, mods[0], w_in[:, cols].astype(BF16), b_in[cols][None, :], cos_e, sin_e)

    def lru_weights(dd):
        w = jnp.concatenate([_block_diag(e_lru_wa[0, dd]), _block_diag(e_lru_wx[0, dd])], axis=1).astype(BF16)
        bias = jnp.concatenate([e_lru_ba[0, dd], e_lru_bx[0, dd]])[None, :]
        return w, bias, e_lru_lambda[0, dd][None, :]

    cw, cb = e_conv_w[0], e_conv_b[0][None, :]
    hf = _lru(u0, None, None, cw, cb, *lru_weights(0), reverse=False, l_lat=l_lat)
    rec = _lru(u0, g0, hf, cw, cb, *lru_weights(1), reverse=True, l_lat=l_lat)
    att = _swa(q0, k0, v0, e_sink[0], l_lat)
    w_r, b_r = _router_weights(moe_w_group[0], moe_b_group[0], moe_w_router[0], moe_b_router[0])
    x1, f, route, counts = _post_mix(rec, att, x, ctx, mods[0], e_w_out[0].astype(BF16), e_b_out[0][None, :],
                                     ln_g[0, 0][None, :], ln_b[0, 0][None, :], w_r, b_r, s_all, l_lat)
    yb, dest = _moe(f, route, counts, moe_w1, moe_w3, moe_w2, 0)
    x2 = _post_ffn(x1, route, dest, yb, mods[0], ln_g[0, 1][None, :], ln_b[0, 1][None, :], l_lat)

    tk = 768 if s_all % 768 == 0 else ROW_TILE
    cos32, sin32, partner32 = _rope_pattern(MLA_ROPE, row_pos, col_pos)
    w_in, b_in = o_w_in[0], o_b_in[0]
    o_pe = MLA_Q_RANK + MLA_KV_RANK
    o_cv = o_pe + MLA_ROPE

    def pad_cols(wm, bv, n):
        return (jnp.zeros((wm.shape[0], n), F32).at[:, :wm.shape[1]].set(wm),
                jnp.zeros((n,), F32).at[:bv.shape[0]].set(bv))

    pe_cols = o_pe + np.arange(MLA_ROPE)
    wa_, ba_ = pad_cols(w_in[:, pe_cols], b_in[pe_cols], LANES)
    wb_, bb_ = pad_cols(w_in[:, o_pe + partner32], b_in[o_pe + partner32], LANES)
    w_odd = jnp.concatenate([w_in[:, :o_pe], wa_, wb_, w_in[:, o_cv:]], axis=1).astype(BF16)
    b_odd = jnp.concatenate([b_in[:o_pe], ba_, bb_, b_in[o_cv:]])[None, :]
    fill = lambda v, n: jnp.full((s_all, n), v, F32)
    ck = jnp.concatenate([cos32, fill(1.0, LANES - MLA_ROPE)], axis=1)
    sk = jnp.concatenate([sin32, fill(0.0, LANES - MLA_ROPE)], axis=1)
    cq = jnp.concatenate([fill(1.0, MLA_NOPE), cos32, fill(1.0, LANES - MLA_NOPE - MLA_ROPE)], axis=1)
    sq = jnp.concatenate([fill(0.0, MLA_NOPE), sin32, fill(0.0, LANES - MLA_NOPE - MLA_ROPE)], axis=1)
    hd = MLA_NOPE + MLA_ROPE
    wuq = o_w_uq[0].reshape(MLA_Q_RANK, MLA_HEADS, hd)
    wuq_main = jnp.zeros((MLA_Q_RANK, MLA_HEADS, LANES), F32).at[:, :, :hd].set(wuq)
    wuq_part = jnp.zeros((MLA_Q_RANK, MLA_HEADS, LANES), F32).at[:, :, MLA_NOPE:hd].set(wuq[:, :, MLA_NOPE + partner32])
    hw = MLA_HEADS * LANES
    wuq_ext = jnp.concatenate([wuq_main.reshape(MLA_Q_RANK, hw), wuq_part.reshape(MLA_Q_RANK, hw)], axis=1).astype(BF16)
    wuk = jnp.zeros((MLA_KV_RANK, MLA_HEADS, LANES), F32).at[:, :, :MLA_NOPE].set(
        o_w_uk[0].reshape(MLA_KV_RANK, MLA_HEADS, MLA_NOPE)).reshape(MLA_KV_RANK, hw).astype(BF16)
    place = np.zeros((LANES, MLA_HEADS, LANES), np.float32)
    for hh in range(MLA_HEADS):
        place[np.arange(MLA_ROPE), hh, MLA_NOPE + np.arange(MLA_ROPE)] = 1.0
    place = jnp.asarray(place.reshape(LANES, hw), BF16)
    wuvt = o_w_uv[0].T.astype(BF16)
    q1, k1, vt1, glu = _in_odd(x2, mods[1], w_odd, b_odd, o_q_norm[0][None, :], o_kv_norm[0][None, :], wuq_ext, wuk,
                               place, wuvt, cq, sq, ck, sk, l_lat, tk)
    att1 = _flash(q1, k1, vt1, l_lat, tk)
    conv1 = _conformer(glu, o_dw_w[0], o_dw_b[0][None, :], o_cln_g[0][None, :], o_cln_b[0][None, :], l_lat)
    w_r, b_r = _router_weights(moe_w_group[1], moe_b_group[1], moe_w_router[1], moe_b_router[1])
    x3, f, route, counts = _post_mix(att1, conv1, x2, x2, mods[1], o_w_out[0].astype(BF16), o_b_out[0][None, :],
                                     ln_g[1, 0][None, :], ln_b[1, 0][None, :], w_r, b_r, l_lat, l_lat)
    yb, dest = _moe(f, route, counts, moe_w1, moe_w3, moe_w2, 1)
    return _post_ffn(x3, route, dest, yb, mods[1], ln_g[1, 1][None, :], ln_b[1, 1][None, :], l_lat)
```
